```python
import jax, jax.numpy as jnp
from jax import lax
import numpy as np

D_MODEL = 1024
BATCH = 4
SEQ = 8192
DEPTH = 4

N_MIXERS = 3
Q_BLOCK = 128
ROPE_THETA = 500000.0
EPS = 1e-6
MAX_POS_OFFSET = 4096

SB_HEADS = 8
SB_HEAD_DIM = D_MODEL // SB_HEADS

MLA_HEADS = 8
MLA_Q_LORA = 256
MLA_KV_LORA = 128
MLA_NOPE_DIM = 128
MLA_ROPE_DIM = 64
MLA_V_DIM = D_MODEL // MLA_HEADS

DSA_HEADS = 8
DSA_KV_HEADS = 2
DSA_HEAD_DIM = 128
DSA_ROT_DIM = DSA_HEAD_DIM // 4
IDX_HEADS = 8
IDX_DIM = 64
IDX_ROT_DIM = IDX_DIM // 4
DSA_TOPK_MAX = 256
DSA_IN_WIDTH = (DSA_HEADS * DSA_HEAD_DIM + 2 * DSA_KV_HEADS * DSA_HEAD_DIM
                + IDX_HEADS * IDX_DIM + IDX_DIM + IDX_HEADS)

D_FF = 2816
N_EXPERTS = 8
TOP_K = 2
D_FF_EXPERT = 1408

kernel_name = 'hybrid_sb_mla_dsa_moe_adaln'


def _n_layers_of(period, phase):
    return len(range(phase, DEPTH, period))


def _rmsnorm(x, g):
    xf = x.astype(jnp.float32)
    y = xf * lax.rsqrt(jnp.mean(xf * xf, axis=-1, keepdims=True) + EPS)
    return (y * g.astype(jnp.float32)).astype(x.dtype)


def _rope_tables(positions, rot_dim):
    inv_freq = ROPE_THETA ** (-jnp.arange(0, rot_dim, 2, dtype=jnp.float32) / rot_dim)
    ang = positions.astype(jnp.float32)[..., None] * inv_freq
    return jnp.cos(ang), jnp.sin(ang)


def _apply_rope(x, cos, sin):
    rot = 2 * cos.shape[-1]
    if x.ndim == 4:
        cos, sin = cos[:, :, None, :], sin[:, :, None, :]
    xr = x[..., :rot].astype(jnp.float32)
    x1, x2 = xr[..., :rot // 2], xr[..., rot // 2:]
    xr = jnp.concatenate([x1 * cos - x2 * sin, x2 * cos + x1 * sin], axis=-1).astype(x.dtype)
    return jnp.concatenate([xr, x[..., rot:]], axis=-1)


def _swiglu(h, w_gu, w_down):
    a, b = jnp.split(h @ w_gu, 2, axis=-1)
    return (jax.nn.silu(a) * b) @ w_down


def _stick_breaking(h, w_in, w_out):
    B, S, _ = h.shape
    q, k, v = jnp.split((h @ w_in).reshape(B, S, 3 * SB_HEADS, SB_HEAD_DIM), 3, axis=2)
    scale = SB_HEAD_DIM ** -0.5
    outs = []
    for start in range(0, S, Q_BLOCK):
        end = start + Q_BLOCK
        z = jnp.einsum('bqhd,bkhd->bhqk', q[:, start:end], k[:, :end]).astype(jnp.float32) * scale
        mask = jnp.arange(end)[None, :] < jnp.arange(start, end)[:, None]
        log_beta = jax.nn.log_sigmoid(z)
        log_fail = jnp.where(mask, jax.nn.log_sigmoid(-z), 0.0)
        tail = lax.cumsum(log_fail, axis=3, reverse=True) - log_fail
        a = jnp.where(mask, jnp.exp(log_beta + tail), 0.0)
        outs.append(jnp.einsum('bhqk,bkhd->bqhd', a.astype(v.dtype), v[:, :end]))
    o = jnp.concatenate(outs, axis=1).reshape(B, S, SB_HEADS * SB_HEAD_DIM)
    return o @ w_out


def _causal_softmax_attention(q, k, v, scale):
    S = q.shape[1]
    outs = []
    for start in range(0, S, Q_BLOCK):
        end = start + Q_BLOCK
        s = jnp.einsum('bqhd,bkhd->bhqk', q[:, start:end], k[:, :end]).astype(jnp.float32) * scale
        mask = jnp.arange(end)[None, :] <= jnp.arange(start, end)[:, None]
        p = jax.nn.softmax(jnp.where(mask, s, -jnp.inf), axis=-1)
        outs.append(jnp.einsum('bhqk,bkhd->bqhd', p.astype(v.dtype), v[:, :end]))
    return jnp.concatenate(outs, axis=1)


def _mla(h, cos, sin, w_in, g_q, w_q_up, g_kv, w_kv_up, w_out):
    B, S, _ = h.shape
    c_q, c_kv, k_rope = jnp.split(h @ w_in, [MLA_Q_LORA, MLA_Q_LORA + MLA_KV_LORA], axis=-1)
    q = (_rmsnorm(c_q, g_q) @ w_q_up).reshape(B, S, MLA_HEADS, MLA_NOPE_DIM + MLA_ROPE_DIM)
    q_nope, q_rope = jnp.split(q, [MLA_NOPE_DIM], axis=-1)
    kv = (_rmsnorm(c_kv, g_kv) @ w_kv_up).reshape(B, S, MLA_HEADS, MLA_NOPE_DIM + MLA_V_DIM)
    k_nope, v = jnp.split(kv, [MLA_NOPE_DIM], axis=-1)
    q_rope = _apply_rope(q_rope, cos, sin)
    k_rope = _apply_rope(k_rope, cos, sin)
    q = jnp.concatenate([q_nope, q_rope], axis=-1)
    k = jnp.concatenate([k_nope, jnp.broadcast_to(k_rope[:, :, None, :], (B, S, MLA_HEADS, MLA_ROPE_DIM))], axis=-1)
    o = _causal_softmax_attention(q, k, v, (MLA_NOPE_DIM + MLA_ROPE_DIM) ** -0.5)
    return o.reshape(B, S, MLA_HEADS * MLA_V_DIM) @ w_out


def _dsa(h, cos_a, sin_a, cos_i, sin_i, w_in, g_kidx, w_out):
    B, S, _ = h.shape
    widths = [DSA_HEADS * DSA_HEAD_DIM, DSA_KV_HEADS * DSA_HEAD_DIM, DSA_KV_HEADS * DSA_HEAD_DIM,
              IDX_HEADS * IDX_DIM, IDX_DIM]
    cuts = [int(i) for i in np.cumsum(widths)]
    q, k, v, q_idx, k_idx, w_idx = jnp.split(h @ w_in, cuts, axis=-1)
    q = _apply_rope(q.reshape(B, S, DSA_HEADS, DSA_HEAD_DIM), cos_a, sin_a)
    k = _apply_rope(k.reshape(B, S, DSA_KV_HEADS, DSA_HEAD_DIM), cos_a, sin_a)
    v = v.reshape(B, S, DSA_KV_HEADS, DSA_HEAD_DIM)
    q_idx = _apply_rope(q_idx.reshape(B, S, IDX_HEADS, IDX_DIM), cos_i, sin_i)
    k_idx = _apply_rope(_rmsnorm(k_idx, g_kidx), cos_i, sin_i)
    w_idx = w_idx.astype(jnp.float32) * (IDX_HEADS ** -0.5)
    n_sel = min(DSA_TOPK_MAX, S // 4)
    group = DSA_HEADS // DSA_KV_HEADS
    gather = jax.vmap(lambda t, i: t[i])
    outs = []
    for start in range(0, S, Q_BLOCK):
        end = start + Q_BLOCK
        n_keys = min(S, max(end, n_sel))
        q_pos = jnp.arange(start, end)
        logits = jnp.einsum('bqhd,bkd->bqhk', q_idx[:, start:end], k_idx[:, :n_keys]).astype(jnp.float32) * IDX_DIM ** -0.5
        score = jnp.einsum('bqhk,bqh->bqk', jax.nn.relu(logits), w_idx[:, start:end])
        score = jnp.where(jnp.arange(n_keys)[None, None, :] <= q_pos[None, :, None], score, -jnp.inf)
        _, sel = lax.top_k(score, n_sel)
        valid = sel <= q_pos[None, :, None]
        k_sel = gather(k, sel)
        v_sel = gather(v, sel)
        qg = q[:, start:end].reshape(B, Q_BLOCK, DSA_KV_HEADS, group, DSA_HEAD_DIM)
        s = jnp.einsum('bqgnd,bqkgd->bqgnk', qg, k_sel).astype(jnp.float32) * DSA_HEAD_DIM ** -0.5
        p = jax.nn.softmax(jnp.where(valid[:, :, None, None, :], s, -jnp.inf), axis=-1)
        o = jnp.einsum('bqgnk,bqkgd->bqgnd', p.astype(v.dtype), v_sel)
        outs.append(o.reshape(B, Q_BLOCK, DSA_HEADS * DSA_HEAD_DIM))
    return jnp.concatenate(outs, axis=1) @ w_out


def _moe(h, w_router, b_router, w_gu, w_down):
    logits = (h @ w_router).astype(jnp.float32) + b_router.astype(jnp.float32)
    top_vals, top_idx = lax.top_k(logits, TOP_K)
    top_w = jax.nn.softmax(top_vals, axis=-1)
    gates = jnp.sum(jax.nn.one_hot(top_idx, N_EXPERTS, dtype=jnp.float32) * top_w[..., None], axis=-2)
    out = jnp.zeros_like(h)
    for e in range(N_EXPERTS):
        out = out + gates[..., e:e + 1].astype(h.dtype) * _swiglu(h, w_gu[e], w_down[e])
    return out


def setup_inputs(seed: int = 0) -> dict:
    key = jax.random.key(seed)
    keys = iter(jax.random.split(key, 40))

    def nrm(shape, scale):
        return jax.random.normal(next(keys), shape, jnp.float32) * scale

    def gain(shape):
        return 1.0 + nrm(shape, 0.02)

    d = D_MODEL
    n_a, n_b, n_c = _n_layers_of(N_MIXERS, 0), _n_layers_of(N_MIXERS, 1), _n_layers_of(N_MIXERS, 2)
    n_dense, n_moe = _n_layers_of(2, 0), _n_layers_of(2, 1)
    x = nrm((BATCH, SEQ, d), 1.0)
    c = nrm((BATCH, d), 1.0)
    offset = jax.random.randint(next(keys), (BATCH, 1), 0, MAX_POS_OFFSET, dtype=jnp.int32)
    positions = offset + jnp.arange(SEQ, dtype=jnp.int32)[None, :]
    return {
        'x': x,
        'c': c,
        'positions': positions,
        'ada_w': nrm((DEPTH, d, 6 * d), 0.5 * d ** -0.5),
        'ada_b': nrm((DEPTH, 6 * d), 0.02),
        'norm_g': gain((DEPTH, 4, d)),
        'sb_w_in': nrm((n_a, d, 3 * SB_HEADS * SB_HEAD_DIM), d ** -0.5),
        'sb_w_out': nrm((n_a, SB_HEADS * SB_HEAD_DIM, d), (SB_HEADS * SB_HEAD_DIM) ** -0.5),
        'mla_w_in': nrm((n_b, d, MLA_Q_LORA + MLA_KV_LORA + MLA_ROPE_DIM), d ** -0.5),
        'mla_g_q': gain((n_b, MLA_Q_LORA)),
        'mla_w_q_up': nrm((n_b, MLA_Q_LORA, MLA_HEADS * (MLA_NOPE_DIM + MLA_ROPE_DIM)), MLA_Q_LORA ** -0.5),
        'mla_g_kv': gain((n_b, MLA_KV_LORA)),
        'mla_w_kv_up': nrm((n_b, MLA_KV_LORA, MLA_HEADS * (MLA_NOPE_DIM + MLA_V_DIM)), MLA_KV_LORA ** -0.5),
        'mla_w_out': nrm((n_b, MLA_HEADS * MLA_V_DIM, d), (MLA_HEADS * MLA_V_DIM) ** -0.5),
        'dsa_w_in': nrm((n_c, d, DSA_IN_WIDTH), d ** -0.5),
        'dsa_g_kidx': gain((n_c, IDX_DIM)),
        'dsa_w_out': nrm((n_c, DSA_HEADS * DSA_HEAD_DIM, d), (DSA_HEADS * DSA_HEAD_DIM) ** -0.5),
        'ffn_w_gu': nrm((n_dense, d, 2 * D_FF), d ** -0.5),
        'ffn_w_down': nrm((n_dense, D_FF, d), D_FF ** -0.5),
        'moe_w_router': nrm((n_moe, d, N_EXPERTS), d ** -0.5),
        'moe_b_router': nrm((n_moe, N_EXPERTS), 0.01),
        'moe_w_gu': nrm((n_moe, N_EXPERTS, d, 2 * D_FF_EXPERT), d ** -0.5),
        'moe_w_down': nrm((n_moe, N_EXPERTS, D_FF_EXPERT, d), D_FF_EXPERT ** -0.5),
    }


def reference(x, c, positions, ada_w, ada_b, norm_g, sb_w_in, sb_w_out, mla_w_in, mla_g_q,
              mla_w_q_up, mla_g_kv, mla_w_kv_up, mla_w_out, dsa_w_in, dsa_g_kidx, dsa_w_out,
              ffn_w_gu, ffn_w_down, moe_w_router, moe_b_router, moe_w_gu, moe_w_down):
    cond = jax.nn.silu(c)
    cos_m, sin_m = _rope_tables(positions, MLA_ROPE_DIM)
    cos_a, sin_a = _rope_tables(positions, DSA_ROT_DIM)
    cos_i, sin_i = _rope_tables(positions, IDX_ROT_DIM)
    counters = [0, 0, 0]
    for layer in range(DEPTH):
        mod = (cond @ ada_w[layer] + ada_b[layer])[:, None, :]
        sh_m, sc_m, g_m, sh_f, sc_f, g_f = jnp.split(mod, 6, axis=-1)
        gn = norm_g[layer]
        h = _rmsnorm(x, gn[0]) * (1.0 + sc_m) + sh_m
        kind = layer % N_MIXERS
        j = counters[kind]
        counters[kind] += 1
        if kind == 0:
            y = _stick_breaking(h, sb_w_in[j], sb_w_out[j])
        elif kind == 1:
            y = _mla(h, cos_m, sin_m, mla_w_in[j], mla_g_q[j], mla_w_q_up[j], mla_g_kv[j],
                     mla_w_kv_up[j], mla_w_out[j])
        else:
            y = _dsa(h, cos_a, sin_a, cos_i, sin_i, dsa_w_in[j], dsa_g_kidx[j], dsa_w_out[j])
        x = x + g_m * _rmsnorm(y, gn[1])
        h = _rmsnorm(x, gn[2]) * (1.0 + sc_f) + sh_f
        f = layer // 2
        if layer % 2 == 0:
            y = _swiglu(h, ffn_w_gu[f], ffn_w_down[f])
        else:
            y = _moe(h, moe_w_router[f], moe_b_router[f], moe_w_gu[f], moe_w_down[f])
        x = x + g_f * _rmsnorm(y, gn[3])
    return x
```

```python
import functools

import jax
import jax.numpy as jnp
from jax import lax
from jax.experimental import pallas as pl
from jax.experimental.pallas import tpu as pltpu

BF = jnp.bfloat16
F32 = jnp.float32
I32 = jnp.int32

D_MODEL = 1024
N_MIXERS = 3
ROPE_THETA = 500000.0
EPS = 1e-6
LANES = 128

SB_HEADS = 8
SB_HEAD_DIM = 128
MLA_HEADS = 8
MLA_Q_LORA = 256
MLA_KV_LORA = 128
MLA_NOPE_DIM = 128
MLA_ROPE_DIM = 64
MLA_V_DIM = 128
MLA_QK_PAD = 256
DSA_HEADS = 8
DSA_KV_HEADS = 2
DSA_HEAD_DIM = 128
DSA_ROT_DIM = 32
IDX_HEADS = 8
IDX_DIM = 64
IDX_ROT_DIM = 16
DSA_TOPK_MAX = 256
D_FF = 2816
N_EXPERTS = 8
D_FF_EXPERT = 1408

VMEM_LIMIT = 56 * 1024 * 1024
INT_MIN = -(2 ** 31)
SB_UNDERFLOW = -104.0
NEG_BIG = -1e30


def _params(sem):
    return pltpu.CompilerParams(dimension_semantics=sem, vmem_limit_bytes=VMEM_LIMIT)


def _dot(a, b):
    return jnp.dot(a, b, preferred_element_type=F32)


def _dot_t(a, b):
    return lax.dot_general(a, b, (((1,), (1,)), ((), ())), preferred_element_type=F32)


def _rms(x, g):
    return x * lax.rsqrt(jnp.mean(x * x, axis=-1, keepdims=True) + EPS) * g


def _norm_mod(x, g, sc, sh):
    return _rms(x, g) * (1.0 + sc) + sh


def _rope(x, cos_t, sin_t, half, group):
    lane = lax.broadcasted_iota(I32, x.shape, 1) % group
    partner = jnp.where(lane < half, pltpu.roll(x, LANES - half, 1), pltpu.roll(x, half, 1))
    return x * cos_t + partner * sin_t


def _ada_kernel(c_ref, w_ref, b_ref, o_ref):
    c = c_ref[...]
    cond = c / (1.0 + jnp.exp(-c))
    o_ref[0] = _dot(cond.astype(BF), w_ref[0].astype(BF)) + b_ref[0]


def _ada(c, ada_w, ada_b):
    depth, d, n = ada_w.shape
    rows = 8
    c_pad = jnp.zeros((rows, d), F32).at[: c.shape[0]].set(c)
    tn = 1536
    out = pl.pallas_call(
        _ada_kernel,
        name="ada_mod",
        out_shape=jax.ShapeDtypeStruct((depth, rows, n), F32),
        grid=(depth, n // tn),
        in_specs=[
            pl.BlockSpec((rows, d), lambda l, j: (0, 0)),
            pl.BlockSpec((1, d, tn), lambda l, j: (l, 0, j)),
            pl.BlockSpec((1, 1, tn), lambda l, j: (l, 0, j)),
        ],
        out_specs=pl.BlockSpec((1, rows, tn), lambda l, j: (l, 0, j)),
        compiler_params=_params(("parallel", "parallel")),
    )(c_pad, ada_w, ada_b.reshape(depth, 1, n))
    return out[:, : c.shape[0]]


def _out_kernel(o_ref, w_ref, x_ref, g_ref, gate_ref, xo_ref):
    y = _dot(o_ref[0], w_ref[...])
    xo_ref[0] = x_ref[0] + gate_ref[0] * _rms(y, g_ref[...])


def _out_proj(o, w, x, g, gate, tm=512):
    b, s, d = x.shape
    k = o.shape[-1]
    row = lambda bi, i: (bi, i, 0)
    return pl.pallas_call(
        _out_kernel,
        name="out_proj",
        out_shape=jax.ShapeDtypeStruct(x.shape, F32),
        grid=(b, s // tm),
        in_specs=[
            pl.BlockSpec((1, tm, k), row),
            pl.BlockSpec((k, d), lambda bi, i: (0, 0)),
            pl.BlockSpec((1, tm, d), row),
            pl.BlockSpec((1, d), lambda bi, i: (0, 0)),
            pl.BlockSpec((1, 1, d), lambda bi, i: (bi, 0, 0)),
        ],
        out_specs=pl.BlockSpec((1, tm, d), row),
        compiler_params=_params(("parallel", "parallel")),
    )(o, w, x, g, gate)


def _sb_in_kernel(x_ref, g_ref, sc_ref, sh_ref, w_ref, o_ref, *, tn, n_q):
    h = _norm_mod(x_ref[0], g_ref[...], sc_ref[0], sh_ref[0]).astype(BF)
    scale = SB_HEAD_DIM ** -0.5
    for j in range(0, w_ref.shape[1], tn):
        y = _dot(h, w_ref[:, j:j + tn])
        if j < n_q:
            y = y * scale
        o_ref[0, :, j:j + tn] = y.astype(BF)


def _sb_in(x, g, sc, sh, w, tm=512):
    b, s, d = x.shape
    n = w.shape[1]
    row = lambda bi, i: (bi, i, 0)
    vec = lambda bi, i: (bi, 0, 0)
    return pl.pallas_call(
        functools.partial(_sb_in_kernel, tn=512, n_q=SB_HEADS * SB_HEAD_DIM),
        name="sb_in",
        out_shape=jax.ShapeDtypeStruct((b, s, n), BF),
        grid=(b, s // tm),
        in_specs=[
            pl.BlockSpec((1, tm, d), row),
            pl.BlockSpec((1, d), lambda bi, i: (0, 0)),
            pl.BlockSpec((1, 1, d), vec),
            pl.BlockSpec((1, 1, d), vec),
            pl.BlockSpec((d, n), lambda bi, i: (0, 0)),
        ],
        out_specs=pl.BlockSpec((1, tm, n), row),
        compiler_params=_params(("parallel", "parallel")),
    )(x, g, sc, sh, w)


def _sb_attn_kernel(q_ref, k_ref, v_ref, o_ref, *, t):
    i = pl.program_id(2)
    q = q_ref[0]
    r_io = lax.broadcasted_iota(I32, (t, t), 0)
    c_io = lax.broadcasted_iota(I32, (t, t), 1)
    after = (r_io > c_io).astype(BF)

    def block(j, run, acc, diagonal):
        start = pl.multiple_of(j * t, t)
        z = _dot_t(q, k_ref[0, pl.ds(start, t), :])
        sp = jnp.log1p(jnp.exp(-jnp.abs(z)))
        log_beta = jnp.minimum(z, 0.0) - sp
        log_fail = -jnp.maximum(z, 0.0) - sp
        if diagonal:
            mask = c_io < r_io
            log_fail = jnp.where(mask, log_fail, 0.0)
        hi = log_fail.astype(BF)
        lo = (log_fail - hi.astype(F32)).astype(BF)
        tail = _dot(hi, after) + _dot(lo, after) + run
        a = jnp.exp(log_beta + tail)
        if diagonal:
            a = jnp.where(mask, a, 0.0)
        acc = acc + _dot(a.astype(BF), v_ref[0, pl.ds(start, t), :])
        run = run + jnp.sum(log_fail, axis=-1, keepdims=True)
        return run, acc

    run, acc = block(i, jnp.zeros((t, 1), F32), jnp.zeros((t, SB_HEAD_DIM), F32), True)

    def cond(carry):
        j, live, _, _ = carry
        return jnp.logical_and(j >= 0, live)

    def body(carry):
        j, _, run, acc = carry
        run, acc = block(j, run, acc, False)
        return j - 1, jnp.max(run) > SB_UNDERFLOW, run, acc

    _, _, _, acc = lax.while_loop(cond, body, (i - 1, jnp.max(run) > SB_UNDERFLOW, run, acc))
    o_ref[0] = acc.astype(BF)


def _sb_attn(qkv, t=256):
    b, s, _ = qkv.shape
    nh = SB_HEADS
    return pl.pallas_call(
        functools.partial(_sb_attn_kernel, t=t),
        name="sb_attn",
        out_shape=jax.ShapeDtypeStruct((b, s, nh * SB_HEAD_DIM), BF),
        grid=(b, nh, s // t),
        in_specs=[
            pl.BlockSpec((1, t, SB_HEAD_DIM), lambda bi, h, i: (bi, i, h)),
            pl.BlockSpec((1, s, SB_HEAD_DIM), lambda bi, h, i: (bi, 0, nh + h)),
            pl.BlockSpec((1, s, SB_HEAD_DIM), lambda bi, h, i: (bi, 0, 2 * nh + h)),
        ],
        out_specs=pl.BlockSpec((1, t, SB_HEAD_DIM), lambda bi, h, i: (bi, i, h)),
        compiler_params=_params(("parallel", "parallel", "arbitrary")),
    )(qkv, qkv, qkv)


def _mla_in_kernel(x_ref, g_ref, sc_ref, sh_ref, w_in_ref, gq_ref, wq_ref, gkv_ref, wkv_ref,
                   cos_ref, sin_ref, q_ref, k_ref, v_ref):
    h = _norm_mod(x_ref[0], g_ref[...], sc_ref[0], sh_ref[0]).astype(BF)
    lat = _dot(h, w_in_ref[...])
    cos_t, sin_t = cos_ref[0], sin_ref[0]
    half = MLA_ROPE_DIM // 2
    c_q = _rms(lat[:, :MLA_Q_LORA], gq_ref[...]).astype(BF)
    c_kv = _rms(lat[:, MLA_Q_LORA:MLA_Q_LORA + MLA_KV_LORA], gkv_ref[...]).astype(BF)
    k_rope = _rope(lat[:, MLA_Q_LORA + MLA_KV_LORA:], cos_t, sin_t, half, LANES).astype(BF)
    scale = (MLA_NOPE_DIM + MLA_ROPE_DIM) ** -0.5
    for hd in range(MLA_HEADS):
        c0 = hd * MLA_QK_PAD
        qh = _dot(c_q, wq_ref[:, c0:c0 + MLA_QK_PAD])
        q_ref[0, :, c0:c0 + LANES] = (qh[:, :LANES] * scale).astype(BF)
        q_ref[0, :, c0 + LANES:c0 + MLA_QK_PAD] = (
            _rope(qh[:, LANES:], cos_t, sin_t, half, LANES) * scale).astype(BF)
        k_ref[0, :, c0:c0 + LANES] = _dot(c_kv, wkv_ref[:, hd * LANES:(hd + 1) * LANES]).astype(BF)
        k_ref[0, :, c0 + LANES:c0 + MLA_QK_PAD] = k_rope
    nk = MLA_HEADS * MLA_NOPE_DIM
    v_ref[0] = _dot(c_kv, wkv_ref[:, nk:]).astype(BF)


def _mla_in(x, g, sc, sh, w_in, g_q, w_q, g_kv, w_kv, cos_t, sin_t, tm=512):
    b, s, d = x.shape
    row = lambda bi, i: (bi, i, 0)
    vec = lambda bi, i: (bi, 0, 0)
    full = lambda bi, i: (0, 0)
    nqk = MLA_HEADS * MLA_QK_PAD
    nv = MLA_HEADS * MLA_V_DIM
    return pl.pallas_call(
        _mla_in_kernel,
        name="mla_in",
        out_shape=(jax.ShapeDtypeStruct((b, s, nqk), BF), jax.ShapeDtypeStruct((b, s, nqk), BF),
                   jax.ShapeDtypeStruct((b, s, nv), BF)),
        grid=(b, s // tm),
        in_specs=[
            pl.BlockSpec((1, tm, d), row),
            pl.BlockSpec((1, d), full),
            pl.BlockSpec((1, 1, d), vec),
            pl.BlockSpec((1, 1, d), vec),
            pl.BlockSpec(w_in.shape, full),
            pl.BlockSpec(g_q.shape, full),
            pl.BlockSpec(w_q.shape, full),
            pl.BlockSpec(g_kv.shape, full),
            pl.BlockSpec(w_kv.shape, full),
            pl.BlockSpec((1, tm, LANES), row),
            pl.BlockSpec((1, tm, LANES), row),
        ],
        out_specs=(pl.BlockSpec((1, tm, nqk), row), pl.BlockSpec((1, tm, nqk), row),
                   pl.BlockSpec((1, tm, nv), row)),
        compiler_params=_params(("parallel", "parallel")),
    )(x, g, sc, sh, w_in, g_q, w_q, g_kv, w_kv, cos_t, sin_t)


def _flash_kernel(q_ref, k_ref, v_ref, o_ref, *, t):
    i = pl.program_id(2)
    q = q_ref[0]
    dv = v_ref.shape[-1]

    def block(j, m, l, acc, diagonal):
        start = pl.multiple_of(j * t, t)
        s = _dot_t(q, k_ref[0, pl.ds(start, t), :])
        if diagonal:
            r_io = lax.broadcasted_iota(I32, (t, t), 0)
            c_io = lax.broadcasted_iota(I32, (t, t), 1)
            s = jnp.where(c_io <= r_io, s, NEG_BIG)
        m_new = jnp.maximum(m, jnp.max(s, axis=-1, keepdims=True))
        p = jnp.exp(s - m_new)
        alpha = jnp.exp(m - m_new)
        l = alpha * l + jnp.sum(p, axis=-1, keepdims=True)
        acc = alpha * acc + _dot(p.astype(BF), v_ref[0, pl.ds(start, t), :])
        return m_new, l, acc

    init = (jnp.full((t, 1), NEG_BIG, F32), jnp.zeros((t, 1), F32), jnp.zeros((t, dv), F32))
    m, l, acc = lax.fori_loop(0, i, lambda j, c: block(j, *c, False), init)
    m, l, acc = block(i, m, l, acc, True)
    o_ref[0] = (acc / l).astype(BF)


def _flash(q, k, v, heads, dk, dv, t=512):
    b, s, _ = q.shape
    return pl.pallas_call(
        functools.partial(_flash_kernel, t=t),
        name="flash_attn",
        out_shape=jax.ShapeDtypeStruct((b, s, heads * dv), BF),
        grid=(b, heads, s // t),
        in_specs=[
            pl.BlockSpec((1, t, dk), lambda bi, h, i: (bi, i, h)),
            pl.BlockSpec((1, s, dk), lambda bi, h, i: (bi, 0, h)),
            pl.BlockSpec((1, s, dv), lambda bi, h, i: (bi, 0, h)),
        ],
        out_specs=pl.BlockSpec((1, t, dv), lambda bi, h, i: (bi, i, h)),
        compiler_params=_params(("parallel", "parallel", "arbitrary")),
    )(q, k, v)


DSA_NQ = DSA_HEADS * DSA_HEAD_DIM
DSA_NKV = DSA_KV_HEADS * DSA_HEAD_DIM
DSA_NQI = IDX_HEADS * IDX_DIM
DSA_COLS = DSA_NQ + 2 * DSA_NKV + DSA_NQI + 2 * LANES


def _dsa_in_kernel(x_ref, g_ref, sc_ref, sh_ref, w_ref, gk_ref, cos_a_ref, sin_a_ref, cos_i_ref,
                   sin_i_ref, q_ref, k_ref, v_ref, qi_ref, ki_ref, wi_ref):
    h = _norm_mod(x_ref[0], g_ref[...], sc_ref[0], sh_ref[0]).astype(BF)
    cos_a, sin_a = cos_a_ref[0], sin_a_ref[0]
    cos_i, sin_i = cos_i_ref[0], sin_i_ref[0]
    scale = DSA_HEAD_DIM ** -0.5
    ha, hi = DSA_ROT_DIM // 2, IDX_ROT_DIM // 2
    for hd in range(DSA_HEADS):
        c0 = hd * LANES
        y = _dot(h, w_ref[:, c0:c0 + LANES])
        q_ref[0, :, c0:c0 + LANES] = (_rope(y, cos_a, sin_a, ha, LANES) * scale).astype(BF)
    for hd in range(DSA_KV_HEADS):
        c0 = hd * LANES
        y = _dot(h, w_ref[:, DSA_NQ + c0:DSA_NQ + c0 + LANES])
        k_ref[0, :, c0:c0 + LANES] = _rope(y, cos_a, sin_a, ha, LANES).astype(BF)
    off = DSA_NQ + DSA_NKV
    v_ref[0] = _dot(h, w_ref[:, off:off + DSA_NKV]).astype(BF)
    off += DSA_NKV
    for p in range(DSA_NQI // LANES):
        c0 = p * LANES
        y = _dot(h, w_ref[:, off + c0:off + c0 + LANES])
        qi_ref[0, :, c0:c0 + LANES] = _rope(y, cos_i, sin_i, hi, IDX_DIM).astype(BF)
    off += DSA_NQI
    y = _dot(h, w_ref[:, off:off + LANES])
    y = y * lax.rsqrt(jnp.sum(y * y, axis=-1, keepdims=True) * (1.0 / IDX_DIM) + EPS) * gk_ref[...]
    y = _rope(y, cos_i, sin_i, hi, IDX_DIM)
    ki_ref[0] = (y + pltpu.roll(y, IDX_DIM, 1)).astype(BF)
    off += LANES
    wi_ref[0] = _dot(h, w_ref[:, off:off + LANES]) * (IDX_HEADS ** -0.5 * IDX_DIM ** -0.5)


def _dsa_in(x, g, sc, sh, w, gk, cos_a, sin_a, cos_i, sin_i, tm=512):
    b, s, d = x.shape
    row = lambda bi, i: (bi, i, 0)
    vec = lambda bi, i: (bi, 0, 0)
    full = lambda bi, i: (0, 0)
    tab = pl.BlockSpec((1, tm, LANES), row)
    shapes = (DSA_NQ, DSA_NKV, DSA_NKV, DSA_NQI, LANES)
    return pl.pallas_call(
        _dsa_in_kernel,
        name="dsa_in",
        out_shape=tuple(jax.ShapeDtypeStruct((b, s, n), BF) for n in shapes)
        + (jax.ShapeDtypeStruct((b, s, LANES), F32),),
        grid=(b, s // tm),
        in_specs=[
            pl.BlockSpec((1, tm, d), row),
            pl.BlockSpec((1, d), full),
            pl.BlockSpec((1, 1, d), vec),
            pl.BlockSpec((1, 1, d), vec),
            pl.BlockSpec(w.shape, full),
            pl.BlockSpec(gk.shape, full),
            tab, tab, tab, tab,
        ],
        out_specs=tuple(pl.BlockSpec((1, tm, n), row) for n in shapes + (LANES,)),
        compiler_params=_params(("parallel", "parallel")),
    )(x, g, sc, sh, w, gk, cos_a, sin_a, cos_i, sin_i)


def _dsa_attn_kernel(q_ref, qi_ref, wi_ref, ki_ref, k_ref, v_ref, o_ref,
                     key_scr, m_scr, l_scr, acc_scr, *, tq, tk, n_sel):
    i = pl.program_id(1)
    n_chunks = (i * tq + tq + tk - 1) // tk
    group = DSA_HEADS // DSA_KV_HEADS
    row_io = lax.broadcasted_iota(I32, (tq, tk), 0) + i * tq
    col_io = lax.broadcasted_iota(I32, (tq, tk), 1)
    lane = lax.broadcasted_iota(I32, (tq, LANES), 1)

    qi = qi_ref[0]
    zero = jnp.zeros_like(qi[:, :LANES])
    stacked = []
    for hd in range(IDX_HEADS):
        pair = qi[:, (hd // 2) * LANES:(hd // 2 + 1) * LANES]
        low = (hd % 2) == 0
        stacked.append(jnp.where((lane < IDX_DIM) == low, pair, zero))
    q_all = jnp.concatenate(stacked, axis=0)
    wi = wi_ref[0]
    w_cols = [wi[:, hd:hd + 1] for hd in range(IDX_HEADS)]

    def score_chunk(c, _):
        start = pl.multiple_of(c * tk, tk)
        logits = _dot_t(q_all, ki_ref[0, pl.ds(start, tk), :])
        score = jnp.zeros((tq, tk), F32)
        for hd in range(IDX_HEADS):
            score = score + jnp.maximum(logits[hd * tq:(hd + 1) * tq], 0.0) * w_cols[hd]
        bits = pltpu.bitcast(score, I32)
        key = jnp.where(bits < 0, bits ^ 0x7FFFFFFF, bits)
        key = jnp.where(score == 0.0, 0, key)
        key = jnp.where(col_io + start <= row_io, key, INT_MIN)
        key_scr[:, pl.ds(start, tk)] = key
        return 0

    lax.fori_loop(0, n_chunks, score_chunk, 0)

    def count(pred):
        def chunk(c, cnt):
            start = pl.multiple_of(c * tk, tk)
            hit = pred(key_scr[:, pl.ds(start, tk)], start).astype(I32)
            for u in range(tk // LANES):
                cnt = cnt + hit[:, u * LANES:(u + 1) * LANES]
            return cnt
        cnt = lax.fori_loop(0, n_chunks, chunk, jnp.zeros((tq, LANES), I32))
        return jnp.sum(cnt, axis=-1, keepdims=True)

    def thr_bit(b, thr):
        cand = thr + lax.shift_left(jnp.int32(1), 31 - b)
        n_ge = count(lambda key, start: key >= cand)
        return jnp.where(n_ge >= n_sel, cand, thr)

    thr = lax.fori_loop(0, 32, thr_bit, jnp.full((tq, 1), INT_MIN, I32))
    n_gt = count(lambda key, start: key > thr)
    n_ge = count(lambda key, start: key >= thr)
    want = n_sel - n_gt
    s_max = key_scr.shape[1]
    tie_bits = s_max.bit_length()

    def tie_search(_):
        def cut_bit(b, cut):
            cand = cut + lax.shift_left(jnp.int32(1), tie_bits - 1 - b)
            n_tie = count(lambda key, start: jnp.logical_and(key == thr, col_io + start < cand))
            return jnp.where(n_tie <= want, cand, cut)
        return lax.fori_loop(0, tie_bits, cut_bit, jnp.zeros((tq, 1), I32))

    crowded = jnp.max(jnp.where(jnp.logical_and(n_ge > n_sel, thr > INT_MIN), 1, 0)) > 0
    cut = lax.cond(crowded, tie_search, lambda _: jnp.full((tq, 1), 2 ** tie_bits - 1, I32), 0)

    q = q_ref[0]
    q_g = [jnp.concatenate([q[:, (g * group + n) * LANES:(g * group + n + 1) * LANES]
                            for n in range(group)], axis=0) for g in range(DSA_KV_HEADS)]
    m_scr[...] = jnp.full(m_scr.shape, NEG_BIG, F32)
    l_scr[...] = jnp.zeros(l_scr.shape, F32)
    acc_scr[...] = jnp.zeros(acc_scr.shape, F32)

    def attn_chunk(c, _):
        start = pl.multiple_of(c * tk, tk)
        key = key_scr[:, pl.ds(start, tk)]
        col = col_io + start
        sel = jnp.logical_or(key > thr, jnp.logical_and(key == thr, col < cut))
        sel = jnp.logical_and(sel, col <= row_io)
        sel = jnp.concatenate([sel.astype(F32)] * group, axis=0) > 0.5
        for g in range(DSA_KV_HEADS):
            s = _dot_t(q_g[g], k_ref[0, pl.ds(start, tk), g * LANES:(g + 1) * LANES])
            s = jnp.where(sel, s, NEG_BIG)
            m_old = m_scr[g]
            m_new = jnp.maximum(m_old, jnp.max(s, axis=-1, keepdims=True))
            p = jnp.where(sel, jnp.exp(s - m_new), 0.0)
            alpha = jnp.exp(m_old - m_new)
            l_scr[g] = alpha * l_scr[g] + jnp.sum(p, axis=-1, keepdims=True)
            acc_scr[g] = alpha * acc_scr[g] + _dot(
                p.astype(BF), v_ref[0, pl.ds(start, tk), g * LANES:(g + 1) * LANES])
            m_scr[g] = m_new
        return 0

    lax.fori_loop(0, n_chunks, attn_chunk, 0)
    for g in range(DSA_KV_HEADS):
        out = acc_scr[g] / l_scr[g]
        for n in range(group):
            hd = g * group + n
            o_ref[0, :, hd * LANES:(hd + 1) * LANES] = out[n * tq:(n + 1) * tq].astype(BF)


def _dsa_attn(q, qi, wi, ki, k, v, tq=128, tk=512):
    b, s, _ = q.shape
    n_sel = min(DSA_TOPK_MAX, s // 4)
    tk = min(tk, s)
    group = DSA_HEADS // DSA_KV_HEADS
    row = lambda bi, i: (bi, i, 0)
    whole = lambda bi, i: (bi, 0, 0)
    return pl.pallas_call(
        functools.partial(_dsa_attn_kernel, tq=tq, tk=tk, n_sel=n_sel),
        name="dsa_attn",
        out_shape=jax.ShapeDtypeStruct((b, s, DSA_NQ), BF),
        grid=(b, s // tq),
        in_specs=[
            pl.BlockSpec((1, tq, DSA_NQ), row),
            pl.BlockSpec((1, tq, DSA_NQI), row),
            pl.BlockSpec((1, tq, LANES), row),
            pl.BlockSpec((1, s, LANES), whole),
            pl.BlockSpec((1, s, DSA_NKV), whole),
            pl.BlockSpec((1, s, DSA_NKV), whole),
        ],
        out_specs=pl.BlockSpec((1, tq, DSA_NQ), row),
        scratch_shapes=[
            pltpu.VMEM((tq, s), I32),
            pltpu.VMEM((DSA_KV_HEADS, group * tq, 1), F32),
            pltpu.VMEM((DSA_KV_HEADS, group * tq, 1), F32),
            pltpu.VMEM((DSA_KV_HEADS, group * tq, DSA_HEAD_DIM), F32),
        ],
        compiler_params=_params(("parallel", "arbitrary")),
    )(q, qi, wi, ki, k, v)


def _ffn_kernel(x_ref, g_ref, sc_ref, sh_ref, wr_ref, br_ref, wg_ref, wu_ref, wd_ref, gp_ref,
                gate_ref, xo_ref, h_scr, gates_scr, acc_scr, *, routed):
    e = pl.program_id(2)
    lane = lax.broadcasted_iota(I32, gates_scr.shape, 1)

    @pl.when(e == 0)
    def _():
        h32 = _norm_mod(x_ref[0], g_ref[...], sc_ref[0], sh_ref[0])
        h = h32.astype(BF)
        h_scr[...] = h
        acc_scr[...] = jnp.zeros(acc_scr.shape, F32)
        if routed:
            h_lo = (h32 - h.astype(F32)).astype(BF)
            logits = _dot(h, wr_ref[0]) + (_dot(h, wr_ref[1]) + _dot(h_lo, wr_ref[0]))
            logits = jnp.where(lane < N_EXPERTS, logits + br_ref[...], -jnp.inf)
            m1 = jnp.max(logits, axis=-1, keepdims=True)
            i1 = jnp.min(jnp.where(logits == m1, lane, LANES), axis=-1, keepdims=True)
            rest = jnp.where(lane == i1, -jnp.inf, logits)
            m2 = jnp.max(rest, axis=-1, keepdims=True)
            i2 = jnp.min(jnp.where(rest == m2, lane, LANES), axis=-1, keepdims=True)
            e2 = jnp.exp(m2 - m1)
            w1 = 1.0 / (1.0 + e2)
            gates_scr[...] = jnp.where(lane == i1, w1, 0.0) + jnp.where(lane == i2, e2 * w1, 0.0)

    h = h_scr[...]
    a = _dot(h, wg_ref[0])
    u = _dot(h, wu_ref[0])
    act = (a / (1.0 + jnp.exp(-a)) * u).astype(BF)
    y = _dot(act, wd_ref[0])
    if routed:
        y = y * jnp.sum(jnp.where(lane == e, gates_scr[...], 0.0), axis=-1, keepdims=True)
    acc_scr[...] += y

    @pl.when(e == pl.num_programs(2) - 1)
    def _():
        xo_ref[0] = x_ref[0] + gate_ref[0] * _rms(acc_scr[...], gp_ref[...])


def _ffn(x, g, sc, sh, w_router, b_router, w_gate, w_up, w_down, g_post, gate, routed, tm=512):
    b, s, d = x.shape
    n_e, _, f = w_gate.shape
    row = lambda bi, i, e: (bi, i, 0)
    vec = lambda bi, i, e: (bi, 0, 0)
    full = lambda bi, i, e: (0, 0)
    slab = lambda bi, i, e: (e, 0, 0)
    return pl.pallas_call(
        functools.partial(_ffn_kernel, routed=routed),
        name="moe" if routed else "ffn",
        out_shape=jax.ShapeDtypeStruct(x.shape, F32),
        grid=(b, s // tm, n_e),
        in_specs=[
            pl.BlockSpec((1, tm, d), row),
            pl.BlockSpec((1, d), full),
            pl.BlockSpec((1, 1, d), vec),
            pl.BlockSpec((1, 1, d), vec),
            pl.BlockSpec((2, d, LANES), lambda bi, i, e: (0, 0, 0)),
            pl.BlockSpec((1, LANES), full),
            pl.BlockSpec((1, d, f), slab),
            pl.BlockSpec((1, d, f), slab),
            pl.BlockSpec((1, f, d), slab),
            pl.BlockSpec((1, d), full),
            pl.BlockSpec((1, 1, d), vec),
        ],
        out_specs=pl.BlockSpec((1, tm, d), row),
        scratch_shapes=[
            pltpu.VMEM((tm, d), BF),
            pltpu.VMEM((tm, LANES), F32),
            pltpu.VMEM((tm, d), F32),
        ],
        compiler_params=_params(("parallel", "parallel", "arbitrary")),
    )(x, g, sc, sh, w_router, b_router, w_gate, w_up, w_down, g_post, gate)


def _rope_tables(positions, rot_dim, group):
    half = rot_dim // 2
    inv_freq = ROPE_THETA ** (-jnp.arange(0, rot_dim, 2, dtype=F32) / rot_dim)
    ang = positions.astype(F32)[..., None] * inv_freq
    cos, sin = jnp.cos(ang), jnp.sin(ang)
    pad = group - rot_dim
    ones = jnp.ones(cos.shape[:-1] + (pad,), F32)
    zeros = jnp.zeros(cos.shape[:-1] + (pad,), F32)
    cos_t = jnp.concatenate([cos, cos, ones], axis=-1)
    sin_t = jnp.concatenate([-sin, sin, zeros], axis=-1)
    reps = LANES // group
    return jnp.tile(cos_t, (1, 1, reps)), jnp.tile(sin_t, (1, 1, reps))


def _pad_cols(w, n):
    return jnp.pad(w, ((0, 0), (0, n - w.shape[1])))


def _split_gu(w_gu, f):
    return w_gu[..., :f].astype(BF), w_gu[..., f:].astype(BF)


def kernel(x, c, positions, ada_w, ada_b, norm_g, sb_w_in, sb_w_out, mla_w_in, mla_g_q, mla_w_q_up,
           mla_g_kv, mla_w_kv_up, mla_w_out, dsa_w_in, dsa_g_kidx, dsa_w_out, ffn_w_gu, ffn_w_down,
           moe_w_router, moe_b_router, moe_w_gu, moe_w_down):
    depth = ada_w.shape[0]
    b, s, d = x.shape
    mod = _ada(c, ada_w, ada_b)
    cos_m, sin_m = _rope_tables(positions, MLA_ROPE_DIM, LANES)
    cos_a, sin_a = _rope_tables(positions, DSA_ROT_DIM, LANES)
    cos_i, sin_i = _rope_tables(positions, IDX_ROT_DIM, IDX_DIM)
    counters = [0, 0, 0]
    for layer in range(depth):
        sh_m, sc_m, g_m, sh_f, sc_f, g_f = [
            mod[layer, :, n * d:(n + 1) * d].reshape(b, 1, d) for n in range(6)]
        gn = norm_g[layer].reshape(4, 1, d)
        kind = layer % N_MIXERS
        j = counters[kind]
        counters[kind] += 1
        if kind == 0:
            qkv = _sb_in(x, gn[0], sc_m, sh_m, sb_w_in[j].astype(BF))
            o = _sb_attn(qkv)
            w_out = sb_w_out[j]
        elif kind == 1:
            w_in = _pad_cols(mla_w_in[j], 4 * LANES).astype(BF)
            w_q = mla_w_q_up[j].reshape(MLA_Q_LORA, MLA_HEADS, MLA_NOPE_DIM + MLA_ROPE_DIM)
            w_q = jnp.pad(w_q, ((0, 0), (0, 0), (0, MLA_QK_PAD - w_q.shape[-1])))
            w_q = w_q.reshape(MLA_Q_LORA, MLA_HEADS * MLA_QK_PAD).astype(BF)
            w_kv = mla_w_kv_up[j].reshape(MLA_KV_LORA, MLA_HEADS, MLA_NOPE_DIM + MLA_V_DIM)
            w_kv = jnp.concatenate([w_kv[..., :MLA_NOPE_DIM].reshape(MLA_KV_LORA, -1),
                                    w_kv[..., MLA_NOPE_DIM:].reshape(MLA_KV_LORA, -1)], axis=1)
            q, k, v = _mla_in(x, gn[0], sc_m, sh_m, w_in, mla_g_q[j].reshape(1, -1), w_q,
                              mla_g_kv[j].reshape(1, -1), w_kv.astype(BF), cos_m, sin_m)
            o = _flash(q, k, v, MLA_HEADS, MLA_QK_PAD, MLA_V_DIM)
            w_out = mla_w_out[j]
        else:
            w = dsa_w_in[j]
            n_main = DSA_NQ + 2 * DSA_NKV + DSA_NQI
            w = jnp.concatenate([w[:, :n_main], _pad_cols(w[:, n_main:n_main + IDX_DIM], LANES),
                                 _pad_cols(w[:, n_main + IDX_DIM:], LANES)], axis=1).astype(BF)
            gk = _pad_cols(dsa_g_kidx[j].reshape(1, -1), LANES)
            q, k, v, qi, ki, wi = _dsa_in(x, gn[0], sc_m, sh_m, w, gk, cos_a, sin_a, cos_i, sin_i)
            o = _dsa_attn(q, qi, wi, ki, k, v)
            w_out = dsa_w_out[j]
        x = _out_proj(o, w_out.astype(BF), x, gn[1], g_m)
        f = layer // 2
        if layer % 2 == 0:
            half = D_FF // 2
            w_gate, w_up = _split_gu(ffn_w_gu[f], D_FF)
            w_gate = w_gate.reshape(d, 2, half).transpose(1, 0, 2)
            w_up = w_up.reshape(d, 2, half).transpose(1, 0, 2)
            w_down = ffn_w_down[f].astype(BF).reshape(2, half, d)
            w_r = jnp.zeros((2, d, LANES), BF)
            b_r = jnp.zeros((1, LANES), F32)
            x = _ffn(x, gn[2], sc_f, sh_f, w_r, b_r, w_gate, w_up, w_down, gn[3], g_f, routed=False)
        else:
            w_gate, w_up = _split_gu(moe_w_gu[f], D_FF_EXPERT)
            w_r = _pad_cols(moe_w_router[f], LANES)
            w_r_hi = w_r.astype(BF)
            w_r = jnp.stack([w_r_hi, (w_r - w_r_hi.astype(F32)).astype(BF)])
            b_r = _pad_cols(moe_b_router[f].reshape(1, -1), LANES)
            x = _ffn(x, gn[2], sc_f, sh_f, w_r, b_r, w_gate, w_up, moe_w_down[f].astype(BF), gn[3],
                     g_f, routed=True)
    return x
```

```python
import functools

import jax
import jax.numpy as jnp
from jax import lax
from jax.experimental import pallas as pl
from jax.experimental.pallas import tpu as pltpu

BF = jnp.bfloat16
F32 = jnp.float32
I32 = jnp.int32

D_MODEL = 1024
N_MIXERS = 3
ROPE_THETA = 500000.0
EPS = 1e-6
LANES = 128

SB_HEADS = 8
SB_HEAD_DIM = 128
MLA_HEADS = 8
MLA_Q_LORA = 256
MLA_KV_LORA = 128
MLA_NOPE_DIM = 128
MLA_ROPE_DIM = 64
MLA_V_DIM = 128
MLA_QK_PAD = 256
DSA_HEADS = 8
DSA_KV_HEADS = 2
DSA_HEAD_DIM = 128
DSA_ROT_DIM = 32
IDX_HEADS = 8
IDX_DIM = 64
IDX_ROT_DIM = 16
DSA_TOPK_MAX = 256
D_FF = 2816
N_EXPERTS = 8
D_FF_EXPERT = 1408

VMEM_LIMIT = 56 * 1024 * 1024
INT_MIN = -(2 ** 31)
SB_UNDERFLOW = -104.0
NEG_BIG = -1e30


def _params(sem):
    return pltpu.CompilerParams(dimension_semantics=sem, vmem_limit_bytes=VMEM_LIMIT)


def _dot(a, b):
    return jnp.dot(a, b, preferred_element_type=F32)


def _dot_t(a, b):
    return lax.dot_general(a, b, (((1,), (1,)), ((), ())), preferred_element_type=F32)


def _rms(x, g):
    return x * lax.rsqrt(jnp.mean(x * x, axis=-1, keepdims=True) + EPS) * g


def _norm_mod(x, g, sc, sh):
    return _rms(x, g) * (1.0 + sc) + sh


def _rope(x, cos_t, sin_t, half, group):
    lane = lax.broadcasted_iota(I32, x.shape, 1) % group
    partner = jnp.where(lane < half, pltpu.roll(x, LANES - half, 1), pltpu.roll(x, half, 1))
    return x * cos_t + partner * sin_t


VT_ONES = 16
VT_ROWS = LANES + VT_ONES
M_INIT = -1e20


def _store_vt(vt_ref, head, v, stage_ref):
    r0 = head * VT_ROWS
    stage_ref[...] = v
    vt_ref[0, r0:r0 + LANES, :] = stage_ref[...].T.astype(BF)
    vt_ref[0, r0 + LANES:r0 + VT_ROWS, :] = jnp.ones((VT_ONES, v.shape[0]), BF)


def _softmax_step(s_t, m, acc, vt):
    m_new = jnp.maximum(m, jnp.max(s_t, axis=0, keepdims=True))
    p = jnp.exp(s_t - m_new).astype(BF)
    acc = jnp.exp(m - m_new) * acc + _dot(vt, p)
    return m_new, acc


def _softmax_finish(acc):
    return (acc[:LANES] / acc[LANES:LANES + 1]).T


def _ada_kernel(c_ref, w_ref, b_ref, o_ref):
    c = c_ref[...]
    cond = c / (1.0 + jnp.exp(-c))
    o_ref[0] = _dot(cond.astype(BF), w_ref[0].astype(BF)) + b_ref[0]


def _ada(c, ada_w, ada_b):
    depth, d, n = ada_w.shape
    rows = 8
    c_pad = jnp.zeros((rows, d), F32).at[: c.shape[0]].set(c)
    tn = 1536
    out = pl.pallas_call(
        _ada_kernel,
        name="ada_mod",
        out_shape=jax.ShapeDtypeStruct((depth, rows, n), F32),
        grid=(depth, n // tn),
        in_specs=[
            pl.BlockSpec((rows, d), lambda l, j: (0, 0)),
            pl.BlockSpec((1, d, tn), lambda l, j: (l, 0, j)),
            pl.BlockSpec((1, 1, tn), lambda l, j: (l, 0, j)),
        ],
        out_specs=pl.BlockSpec((1, rows, tn), lambda l, j: (l, 0, j)),
        compiler_params=_params(("parallel", "parallel")),
    )(c_pad, ada_w, ada_b.reshape(depth, 1, n))
    return out[:, : c.shape[0]]


def _out_kernel(o_ref, w_ref, x_ref, g_ref, gate_ref, xo_ref):
    y = _dot(o_ref[0], w_ref[...])
    xo_ref[0] = x_ref[0] + gate_ref[0] * _rms(y, g_ref[...])


def _out_proj(o, w, x, g, gate, tm=512):
    b, s, d = x.shape
    k = o.shape[-1]
    row = lambda bi, i: (bi, i, 0)
    return pl.pallas_call(
        _out_kernel,
        name="out_proj",
        out_shape=jax.ShapeDtypeStruct(x.shape, F32),
        grid=(b, s // tm),
        in_specs=[
            pl.BlockSpec((1, tm, k), row),
            pl.BlockSpec((k, d), lambda bi, i: (0, 0)),
            pl.BlockSpec((1, tm, d), row),
            pl.BlockSpec((1, d), lambda bi, i: (0, 0)),
            pl.BlockSpec((1, 1, d), lambda bi, i: (bi, 0, 0)),
        ],
        out_specs=pl.BlockSpec((1, tm, d), row),
        compiler_params=_params(("parallel", "parallel")),
    )(o, w, x, g, gate)


def _sb_in_kernel(x_ref, g_ref, sc_ref, sh_ref, w_ref, o_ref, *, tn, n_q):
    h = _norm_mod(x_ref[0], g_ref[...], sc_ref[0], sh_ref[0]).astype(BF)
    scale = SB_HEAD_DIM ** -0.5
    for j in range(0, w_ref.shape[1], tn):
        y = _dot(h, w_ref[:, j:j + tn])
        if j < n_q:
            y = y * scale
        o_ref[0, :, j:j + tn] = y.astype(BF)


def _sb_in(x, g, sc, sh, w, tm=512):
    b, s, d = x.shape
    n = w.shape[1]
    row = lambda bi, i: (bi, i, 0)
    vec = lambda bi, i: (bi, 0, 0)
    return pl.pallas_call(
        functools.partial(_sb_in_kernel, tn=512, n_q=SB_HEADS * SB_HEAD_DIM),
        name="sb_in",
        out_shape=jax.ShapeDtypeStruct((b, s, n), BF),
        grid=(b, s // tm),
        in_specs=[
            pl.BlockSpec((1, tm, d), row),
            pl.BlockSpec((1, d), lambda bi, i: (0, 0)),
            pl.BlockSpec((1, 1, d), vec),
            pl.BlockSpec((1, 1, d), vec),
            pl.BlockSpec((d, n), lambda bi, i: (0, 0)),
        ],
        out_specs=pl.BlockSpec((1, tm, n), row),
        compiler_params=_params(("parallel", "parallel")),
    )(x, g, sc, sh, w)


def _sb_attn_kernel(q_ref, k_ref, v_ref, o_ref, *, t, heads):
    i = pl.program_id(2)
    d = SB_HEAD_DIM
    r_io = lax.broadcasted_iota(I32, (t, t + LANES), 0)
    c_io = lax.broadcasted_iota(I32, (t, t + LANES), 1)
    after = jnp.logical_or(r_io > c_io, c_io >= t).astype(BF)
    mask = lax.broadcasted_iota(I32, (t, t), 1) < lax.broadcasted_iota(I32, (t, t), 0)

    def block(j, hd, run, acc, diagonal):
        start = pl.multiple_of(j * t, t)
        cols = slice(hd * d, (hd + 1) * d)
        z = _dot_t(q_ref[0, :, cols], k_ref[0, pl.ds(start, t), cols])
        sp = jnp.log1p(jnp.exp(-jnp.abs(z)))
        log_beta = jnp.minimum(z, 0.0) - sp
        log_fail = -jnp.maximum(z, 0.0) - sp
        if diagonal:
            log_fail = jnp.where(mask, log_fail, 0.0)
        hi = log_fail.astype(BF)
        lo = (log_fail - hi.astype(F32)).astype(BF)
        sums = _dot(hi, after) + _dot(lo, after)
        tail = jnp.concatenate(
            [sums[:, u * LANES:(u + 1) * LANES] + run for u in range(t // LANES)], axis=1)
        a = jnp.exp(log_beta + tail)
        if diagonal:
            a = jnp.where(mask, a, 0.0)
        acc = acc + _dot(a.astype(BF), v_ref[0, pl.ds(start, t), cols])
        return run + sums[:, t:], acc

    def blocks(j, state, diagonal):
        out = []
        for hd in range(heads):
            out.extend(block(j, hd, state[2 * hd], state[2 * hd + 1], diagonal))
        return tuple(out)

    def live(state):
        top = state[0]
        for hd in range(1, heads):
            top = jnp.maximum(top, state[2 * hd])
        return jnp.max(top) > SB_UNDERFLOW

    state = blocks(i, (jnp.zeros((t, LANES), F32), jnp.zeros((t, d), F32)) * heads, True)

    def cond(carry):
        return jnp.logical_and(carry[0] >= 0, carry[1])

    def body(carry):
        state = blocks(carry[0], carry[2:], False)
        return (carry[0] - 1, live(state)) + state

    res = lax.while_loop(cond, body, (i - 1, live(state)) + state)
    for hd in range(heads):
        o_ref[0, :, hd * d:(hd + 1) * d] = res[2 + 2 * hd + 1].astype(BF)


def _sb_attn(qkv, t=256, heads=2):
    b, s, _ = qkv.shape
    ng = SB_HEADS // heads
    w = heads * SB_HEAD_DIM
    return pl.pallas_call(
        functools.partial(_sb_attn_kernel, t=t, heads=heads),
        name="sb_attn",
        out_shape=jax.ShapeDtypeStruct((b, s, SB_HEADS * SB_HEAD_DIM), BF),
        grid=(b, ng, s // t),
        in_specs=[
            pl.BlockSpec((1, t, w), lambda bi, h, i: (bi, i, h)),
            pl.BlockSpec((1, s, w), lambda bi, h, i: (bi, 0, ng + h)),
            pl.BlockSpec((1, s, w), lambda bi, h, i: (bi, 0, 2 * ng + h)),
        ],
        out_specs=pl.BlockSpec((1, t, w), lambda bi, h, i: (bi, i, h)),
        compiler_params=_params(("parallel", "parallel", "arbitrary")),
    )(qkv, qkv, qkv)


def _mla_in_kernel(x_ref, g_ref, sc_ref, sh_ref, w_in_ref, gq_ref, wq_ref, gkv_ref, wkv_ref,
                   cos_ref, sin_ref, q_ref, k_ref, v_ref, stage_ref):
    h = _norm_mod(x_ref[0], g_ref[...], sc_ref[0], sh_ref[0]).astype(BF)
    lat = _dot(h, w_in_ref[...])
    cos_t, sin_t = cos_ref[0], sin_ref[0]
    half = MLA_ROPE_DIM // 2
    c_q = _rms(lat[:, :MLA_Q_LORA], gq_ref[...]).astype(BF)
    c_kv = _rms(lat[:, MLA_Q_LORA:MLA_Q_LORA + MLA_KV_LORA], gkv_ref[...]).astype(BF)
    k_rope = _rope(lat[:, MLA_Q_LORA + MLA_KV_LORA:], cos_t, sin_t, half, LANES).astype(BF)
    scale = (MLA_NOPE_DIM + MLA_ROPE_DIM) ** -0.5
    for hd in range(MLA_HEADS):
        c0 = hd * MLA_QK_PAD
        qh = _dot(c_q, wq_ref[:, c0:c0 + MLA_QK_PAD])
        q_ref[0, :, c0:c0 + LANES] = (qh[:, :LANES] * scale).astype(BF)
        q_ref[0, :, c0 + LANES:c0 + MLA_QK_PAD] = (
            _rope(qh[:, LANES:], cos_t, sin_t, half, LANES) * scale).astype(BF)
        k_ref[0, :, c0:c0 + LANES] = _dot(c_kv, wkv_ref[:, hd * LANES:(hd + 1) * LANES]).astype(BF)
        k_ref[0, :, c0 + LANES:c0 + MLA_QK_PAD] = k_rope
    nk = MLA_HEADS * MLA_NOPE_DIM
    for hd in range(MLA_HEADS):
        v = _dot(c_kv, wkv_ref[:, nk + hd * LANES:nk + (hd + 1) * LANES])
        _store_vt(v_ref, hd, v, stage_ref)


def _mla_in(x, g, sc, sh, w_in, g_q, w_q, g_kv, w_kv, cos_t, sin_t, tm=512):
    b, s, d = x.shape
    row = lambda bi, i: (bi, i, 0)
    vec = lambda bi, i: (bi, 0, 0)
    full = lambda bi, i: (0, 0)
    nqk = MLA_HEADS * MLA_QK_PAD
    nv = MLA_HEADS * VT_ROWS
    return pl.pallas_call(
        _mla_in_kernel,
        name="mla_in",
        out_shape=(jax.ShapeDtypeStruct((b, s, nqk), BF), jax.ShapeDtypeStruct((b, s, nqk), BF),
                   jax.ShapeDtypeStruct((b, nv, s), BF)),
        grid=(b, s // tm),
        in_specs=[
            pl.BlockSpec((1, tm, d), row),
            pl.BlockSpec((1, d), full),
            pl.BlockSpec((1, 1, d), vec),
            pl.BlockSpec((1, 1, d), vec),
            pl.BlockSpec(w_in.shape, full),
            pl.BlockSpec(g_q.shape, full),
            pl.BlockSpec(w_q.shape, full),
            pl.BlockSpec(g_kv.shape, full),
            pl.BlockSpec(w_kv.shape, full),
            pl.BlockSpec((1, tm, LANES), row),
            pl.BlockSpec((1, tm, LANES), row),
        ],
        out_specs=(pl.BlockSpec((1, tm, nqk), row), pl.BlockSpec((1, tm, nqk), row),
                   pl.BlockSpec((1, nv, tm), lambda bi, i: (bi, 0, i))),
        scratch_shapes=[pltpu.VMEM((tm, LANES), F32)],
        compiler_params=_params(("parallel", "parallel")),
    )(x, g, sc, sh, w_in, g_q, w_q, g_kv, w_kv, cos_t, sin_t)


def _flash_kernel(q_ref, k_ref, vt_ref, o_ref, *, t):
    i = pl.program_id(2)
    q = q_ref[0]

    def block(j, m, acc, diagonal):
        start = pl.multiple_of(j * t, t)
        s_t = _dot_t(k_ref[0, pl.ds(start, t), :], q)
        if diagonal:
            key_io = lax.broadcasted_iota(I32, (t, t), 0)
            qry_io = lax.broadcasted_iota(I32, (t, t), 1)
            s_t = jnp.where(key_io <= qry_io, s_t, NEG_BIG)
        return _softmax_step(s_t, m, acc, vt_ref[0, :, pl.ds(start, t)])

    init = (jnp.full((1, t), M_INIT, F32), jnp.zeros((VT_ROWS, t), F32))
    m, acc = lax.fori_loop(0, i, lambda j, c: block(j, *c, False), init)
    m, acc = block(i, m, acc, True)
    o_ref[0] = _softmax_finish(acc).astype(BF)


def _flash(q, k, vt, heads, dk, t=512):
    b, s, _ = q.shape
    return pl.pallas_call(
        functools.partial(_flash_kernel, t=t),
        name="flash_attn",
        out_shape=jax.ShapeDtypeStruct((b, s, heads * LANES), BF),
        grid=(b, heads, s // t),
        in_specs=[
            pl.BlockSpec((1, t, dk), lambda bi, h, i: (bi, i, h)),
            pl.BlockSpec((1, s, dk), lambda bi, h, i: (bi, 0, h)),
            pl.BlockSpec((1, VT_ROWS, s), lambda bi, h, i: (bi, h, 0)),
        ],
        out_specs=pl.BlockSpec((1, t, LANES), lambda bi, h, i: (bi, i, h)),
        compiler_params=_params(("parallel", "parallel", "arbitrary")),
    )(q, k, vt)


DSA_NQ = DSA_HEADS * DSA_HEAD_DIM
DSA_NKV = DSA_KV_HEADS * DSA_HEAD_DIM
DSA_NQI = IDX_HEADS * IDX_DIM
DSA_COLS = DSA_NQ + 2 * DSA_NKV + DSA_NQI + 2 * LANES


def _dsa_in_kernel(x_ref, g_ref, sc_ref, sh_ref, w_ref, gk_ref, cos_a_ref, sin_a_ref, cos_i_ref,
                   sin_i_ref, q_ref, k_ref, v_ref, qi_ref, ki_ref, wi_ref, stage_ref):
    h = _norm_mod(x_ref[0], g_ref[...], sc_ref[0], sh_ref[0]).astype(BF)
    cos_a, sin_a = cos_a_ref[0], sin_a_ref[0]
    cos_i, sin_i = cos_i_ref[0], sin_i_ref[0]
    scale = DSA_HEAD_DIM ** -0.5
    ha, hi = DSA_ROT_DIM // 2, IDX_ROT_DIM // 2
    for hd in range(DSA_HEADS):
        c0 = hd * LANES
        y = _dot(h, w_ref[:, c0:c0 + LANES])
        q_ref[0, :, c0:c0 + LANES] = (_rope(y, cos_a, sin_a, ha, LANES) * scale).astype(BF)
    for hd in range(DSA_KV_HEADS):
        c0 = hd * LANES
        y = _dot(h, w_ref[:, DSA_NQ + c0:DSA_NQ + c0 + LANES])
        k_ref[0, :, c0:c0 + LANES] = _rope(y, cos_a, sin_a, ha, LANES).astype(BF)
    off = DSA_NQ + DSA_NKV
    for hd in range(DSA_KV_HEADS):
        _store_vt(v_ref, hd, _dot(h, w_ref[:, off + hd * LANES:off + (hd + 1) * LANES]), stage_ref)
    off += DSA_NKV
    for p in range(DSA_NQI // LANES):
        c0 = p * LANES
        y = _dot(h, w_ref[:, off + c0:off + c0 + LANES])
        qi_ref[0, :, c0:c0 + LANES] = _rope(y, cos_i, sin_i, hi, IDX_DIM).astype(BF)
    off += DSA_NQI
    y = _dot(h, w_ref[:, off:off + LANES])
    y = y * lax.rsqrt(jnp.sum(y * y, axis=-1, keepdims=True) * (1.0 / IDX_DIM) + EPS) * gk_ref[...]
    y = _rope(y, cos_i, sin_i, hi, IDX_DIM)
    ki_ref[0] = (y + pltpu.roll(y, IDX_DIM, 1)).astype(BF)
    off += LANES
    wi_ref[0] = _dot(h, w_ref[:, off:off + LANES]) * (IDX_HEADS ** -0.5 * IDX_DIM ** -0.5)


def _dsa_in(x, g, sc, sh, w, gk, cos_a, sin_a, cos_i, sin_i, tm=512):
    b, s, d = x.shape
    row = lambda bi, i: (bi, i, 0)
    vec = lambda bi, i: (bi, 0, 0)
    full = lambda bi, i: (0, 0)
    tab = pl.BlockSpec((1, tm, LANES), row)
    nvt = DSA_KV_HEADS * VT_ROWS

    def spec(n):
        return pl.BlockSpec((1, tm, n), row)

    def sds(n, dt=BF):
        return jax.ShapeDtypeStruct((b, s, n), dt)

    return pl.pallas_call(
        _dsa_in_kernel,
        name="dsa_in",
        out_shape=(sds(DSA_NQ), sds(DSA_NKV), jax.ShapeDtypeStruct((b, nvt, s), BF), sds(DSA_NQI),
                   sds(LANES), sds(LANES, F32)),
        grid=(b, s // tm),
        in_specs=[
            pl.BlockSpec((1, tm, d), row),
            pl.BlockSpec((1, d), full),
            pl.BlockSpec((1, 1, d), vec),
            pl.BlockSpec((1, 1, d), vec),
            pl.BlockSpec(w.shape, full),
            pl.BlockSpec(gk.shape, full),
            tab, tab, tab, tab,
        ],
        out_specs=(spec(DSA_NQ), spec(DSA_NKV), pl.BlockSpec((1, nvt, tm), lambda bi, i: (bi, 0, i)),
                   spec(DSA_NQI), spec(LANES), spec(LANES)),
        scratch_shapes=[pltpu.VMEM((tm, LANES), F32)],
        compiler_params=_params(("parallel", "parallel")),
    )(x, g, sc, sh, w, gk, cos_a, sin_a, cos_i, sin_i)


def _dsa_attn_kernel(q_ref, qi_ref, wi_ref, ki_ref, k_ref, vt_ref, o_ref, key_scr, *, tq, tk, n_sel):
    i = pl.program_id(1)
    n_chunks = (i * tq + tq + tk - 1) // tk
    group = DSA_HEADS // DSA_KV_HEADS
    key_io = lax.broadcasted_iota(I32, (tk, tq), 0)
    qry = lax.broadcasted_iota(I32, (1, tq), 1) + i * tq
    lane = lax.broadcasted_iota(I32, (tq, LANES), 1)

    qi = qi_ref[0]
    zero = jnp.zeros_like(qi[:, :LANES])
    stacked = []
    for hd in range(IDX_HEADS):
        pair = qi[:, (hd // 2) * LANES:(hd // 2 + 1) * LANES]
        low = (hd % 2) == 0
        stacked.append(jnp.where((lane < IDX_DIM) == low, pair, zero))
    q_all = jnp.concatenate(stacked, axis=0)
    w_t = wi_ref[0].T

    def score_chunk(c, _):
        start = pl.multiple_of(c * tk, tk)
        logits = _dot_t(ki_ref[0, pl.ds(start, tk), :], q_all)
        score = jnp.zeros((tk, tq), F32)
        for hd in range(IDX_HEADS):
            score = score + jnp.maximum(logits[:, hd * tq:(hd + 1) * tq], 0.0) * w_t[hd:hd + 1]
        bits = pltpu.bitcast(score, I32)
        key = jnp.where(bits < 0, bits ^ 0x7FFFFFFF, bits)
        key = jnp.where(score == 0.0, 0, key)
        key = jnp.where(key_io + start <= qry, key, INT_MIN)
        key_scr[pl.ds(start, tk), :] = key
        return 0

    lax.fori_loop(0, n_chunks, score_chunk, 0)

    def count(pred):
        def chunk(c, cnt):
            start = pl.multiple_of(c * tk, tk)
            key = key_scr[pl.ds(start, tk), :].reshape(tk // 8, 8, tq)
            krow = (key_io + start).reshape(tk // 8, 8, tq)
            return cnt + jnp.sum(pred(key, krow).astype(I32), axis=0)
        cnt = lax.fori_loop(0, n_chunks, chunk, jnp.zeros((8, tq), I32))
        for shift in (4, 2, 1):
            cnt = cnt + pltpu.roll(cnt, shift, 0)
        return cnt

    bits_per_check = 4

    def search_cond(carry):
        b, _, _, done = carry
        return jnp.logical_and(b < 32, jnp.min(done) == 0)

    def search_bits(carry):
        b, thr, n_at, done = carry
        for u in range(bits_per_check):
            cand = thr + lax.shift_left(jnp.int32(1), 31 - (b + u))
            n_ge = count(lambda key, krow: key >= cand)
            take = jnp.logical_and(n_ge >= n_sel, done == 0)
            thr = jnp.where(take, cand, thr)
            n_at = jnp.where(take, n_ge, n_at)
            done = jnp.where(n_at == n_sel, 1, done)
        return b + bits_per_check, thr, n_at, done

    thr0 = jnp.full((8, tq), INT_MIN, I32)
    n_at0 = jnp.zeros((8, tq), I32) + n_chunks * tk
    done0 = jnp.where(jnp.logical_or(qry < n_sel, n_at0 == n_sel), 1, 0)
    _, thr, n_at, _ = lax.while_loop(search_cond, search_bits, (jnp.int32(0), thr0, n_at0, done0))
    tie_bits = key_scr.shape[0].bit_length()

    def tie_search(_):
        want = n_sel - count(lambda key, krow: key > thr)

        def cut_bit(b, cut):
            cand = cut + lax.shift_left(jnp.int32(1), tie_bits - 1 - b)
            n_tie = count(lambda key, krow: jnp.logical_and(key == thr, krow < cand))
            return jnp.where(n_tie <= want, cand, cut)
        return lax.fori_loop(0, tie_bits, cut_bit, jnp.zeros((8, tq), I32))

    crowded = jnp.max(jnp.where(jnp.logical_and(n_at > n_sel, thr > INT_MIN), 1, 0)) > 0
    cut = lax.cond(crowded, tie_search, lambda _: jnp.full((8, tq), 2 ** tie_bits - 1, I32), 0)
    thr, cut = thr[:1], cut[:1]

    q = q_ref[0]
    q_g = [jnp.concatenate([q[:, (g * group + n) * LANES:(g * group + n + 1) * LANES]
                            for n in range(group)], axis=0) for g in range(DSA_KV_HEADS)]

    def attn_chunk(c, carry):
        start = pl.multiple_of(c * tk, tk)
        key = key_scr[pl.ds(start, tk), :]
        krow = key_io + start
        sel = jnp.logical_or(key > thr, jnp.logical_and(key == thr, krow < cut))
        sel = jnp.logical_and(sel, krow <= qry)
        bias = jnp.where(sel, 0.0, NEG_BIG)
        bias = jnp.concatenate([bias] * group, axis=1)
        out = []
        for g in range(DSA_KV_HEADS):
            m, acc = carry[2 * g], carry[2 * g + 1]
            s_t = _dot_t(k_ref[0, pl.ds(start, tk), g * LANES:(g + 1) * LANES], q_g[g]) + bias
            out.extend(_softmax_step(
                s_t, m, acc, vt_ref[0, g * VT_ROWS:(g + 1) * VT_ROWS, pl.ds(start, tk)]))
        return tuple(out)

    init = (jnp.full((1, group * tq), M_INIT, F32), jnp.zeros((VT_ROWS, group * tq), F32))
    res = lax.fori_loop(0, n_chunks, attn_chunk, init * DSA_KV_HEADS)
    for g in range(DSA_KV_HEADS):
        acc = res[2 * g + 1]
        for n in range(group):
            hd = g * group + n
            o_ref[0, :, hd * LANES:(hd + 1) * LANES] = _softmax_finish(
                acc[:, n * tq:(n + 1) * tq]).astype(BF)


def _dsa_attn(q, qi, wi, ki, k, vt, tq=128, tk=512):
    b, s, _ = q.shape
    n_sel = min(DSA_TOPK_MAX, s // 4)
    tk = min(tk, s)
    row = lambda bi, i: (bi, i, 0)
    whole = lambda bi, i: (bi, 0, 0)
    return pl.pallas_call(
        functools.partial(_dsa_attn_kernel, tq=tq, tk=tk, n_sel=n_sel),
        name="dsa_attn",
        out_shape=jax.ShapeDtypeStruct((b, s, DSA_NQ), BF),
        grid=(b, s // tq),
        in_specs=[
            pl.BlockSpec((1, tq, DSA_NQ), row),
            pl.BlockSpec((1, tq, DSA_NQI), row),
            pl.BlockSpec((1, tq, LANES), row),
            pl.BlockSpec((1, s, LANES), whole),
            pl.BlockSpec((1, s, DSA_NKV), whole),
            pl.BlockSpec((1, DSA_KV_HEADS * VT_ROWS, s), whole),
        ],
        out_specs=pl.BlockSpec((1, tq, DSA_NQ), row),
        scratch_shapes=[pltpu.VMEM((s, tq), I32)],
        compiler_params=_params(("parallel", "arbitrary")),
    )(q, qi, wi, ki, k, vt)


def _ffn_kernel(x_ref, g_ref, sc_ref, sh_ref, wr_ref, br_ref, wg_ref, wu_ref, wd_ref, gp_ref,
                gate_ref, xo_ref, h_scr, gates_scr, acc_scr, *, routed):
    e = pl.program_id(2)
    lane = lax.broadcasted_iota(I32, gates_scr.shape, 1)

    @pl.when(e == 0)
    def _():
        h32 = _norm_mod(x_ref[0], g_ref[...], sc_ref[0], sh_ref[0])
        h = h32.astype(BF)
        h_scr[...] = h
        acc_scr[...] = jnp.zeros(acc_scr.shape, F32)
        if routed:
            h_lo = (h32 - h.astype(F32)).astype(BF)
            logits = _dot(h, wr_ref[0]) + (_dot(h, wr_ref[1]) + _dot(h_lo, wr_ref[0]))
            logits = jnp.where(lane < N_EXPERTS, logits + br_ref[...], -jnp.inf)
            m1 = jnp.max(logits, axis=-1, keepdims=True)
            i1 = jnp.min(jnp.where(logits == m1, lane, LANES), axis=-1, keepdims=True)
            rest = jnp.where(lane == i1, -jnp.inf, logits)
            m2 = jnp.max(rest, axis=-1, keepdims=True)
            i2 = jnp.min(jnp.where(rest == m2, lane, LANES), axis=-1, keepdims=True)
            e2 = jnp.exp(m2 - m1)
            w1 = 1.0 / (1.0 + e2)
            gates_scr[...] = jnp.where(lane == i1, w1, 0.0) + jnp.where(lane == i2, e2 * w1, 0.0)

    h = h_scr[...]
    a = _dot(h, wg_ref[0])
    u = _dot(h, wu_ref[0])
    act = (a / (1.0 + jnp.exp(-a)) * u).astype(BF)
    y = _dot(act, wd_ref[0])
    if routed:
        y = y * jnp.sum(jnp.where(lane == e, gates_scr[...], 0.0), axis=-1, keepdims=True)
    acc_scr[...] += y

    @pl.when(e == pl.num_programs(2) - 1)
    def _():
        xo_ref[0] = x_ref[0] + gate_ref[0] * _rms(acc_scr[...], gp_ref[...])


def _ffn(x, g, sc, sh, w_router, b_router, w_gate, w_up, w_down, g_post, gate, routed, tm=512):
    b, s, d = x.shape
    n_e, _, f = w_gate.shape
    row = lambda bi, i, e: (bi, i, 0)
    vec = lambda bi, i, e: (bi, 0, 0)
    full = lambda bi, i, e: (0, 0)
    slab = lambda bi, i, e: (e, 0, 0)
    return pl.pallas_call(
        functools.partial(_ffn_kernel, routed=routed),
        name="moe" if routed else "ffn",
        out_shape=jax.ShapeDtypeStruct(x.shape, F32),
        grid=(b, s // tm, n_e),
        in_specs=[
            pl.BlockSpec((1, tm, d), row),
            pl.BlockSpec((1, d), full),
            pl.BlockSpec((1, 1, d), vec),
            pl.BlockSpec((1, 1, d), vec),
            pl.BlockSpec((2, d, LANES), lambda bi, i, e: (0, 0, 0)),
            pl.BlockSpec((1, LANES), full),
            pl.BlockSpec((1, d, f), slab),
            pl.BlockSpec((1, d, f), slab),
            pl.BlockSpec((1, f, d), slab),
            pl.BlockSpec((1, d), full),
            pl.BlockSpec((1, 1, d), vec),
        ],
        out_specs=pl.BlockSpec((1, tm, d), row),
        scratch_shapes=[
            pltpu.VMEM((tm, d), BF),
            pltpu.VMEM((tm, LANES), F32),
            pltpu.VMEM((tm, d), F32),
        ],
        compiler_params=_params(("parallel", "parallel", "arbitrary")),
    )(x, g, sc, sh, w_router, b_router, w_gate, w_up, w_down, g_post, gate)


def _rope_tables(positions, rot_dim, group):
    half = rot_dim // 2
    inv_freq = ROPE_THETA ** (-jnp.arange(0, rot_dim, 2, dtype=F32) / rot_dim)
    ang = positions.astype(F32)[..., None] * inv_freq
    cos, sin = jnp.cos(ang), jnp.sin(ang)
    pad = group - rot_dim
    ones = jnp.ones(cos.shape[:-1] + (pad,), F32)
    zeros = jnp.zeros(cos.shape[:-1] + (pad,), F32)
    cos_t = jnp.concatenate([cos, cos, ones], axis=-1)
    sin_t = jnp.concatenate([-sin, sin, zeros], axis=-1)
    reps = LANES // group
    return jnp.tile(cos_t, (1, 1, reps)), jnp.tile(sin_t, (1, 1, reps))


def _pad_cols(w, n):
    return jnp.pad(w, ((0, 0), (0, n - w.shape[1])))


def _split_gu(w_gu, f):
    return w_gu[..., :f].astype(BF), w_gu[..., f:].astype(BF)


def kernel(x, c, positions, ada_w, ada_b, norm_g, sb_w_in, sb_w_out, mla_w_in, mla_g_q, mla_w_q_up,
           mla_g_kv, mla_w_kv_up, mla_w_out, dsa_w_in, dsa_g_kidx, dsa_w_out, ffn_w_gu, ffn_w_down,
           moe_w_router, moe_b_router, moe_w_gu, moe_w_down):
    depth = ada_w.shape[0]
    b, s, d = x.shape
    mod = _ada(c, ada_w, ada_b)
    cos_m, sin_m = _rope_tables(positions, MLA_ROPE_DIM, LANES)
    cos_a, sin_a = _rope_tables(positions, DSA_ROT_DIM, LANES)
    cos_i, sin_i = _rope_tables(positions, IDX_ROT_DIM, IDX_DIM)
    counters = [0, 0, 0]
    for layer in range(depth):
        sh_m, sc_m, g_m, sh_f, sc_f, g_f = [
            mod[layer, :, n * d:(n + 1) * d].reshape(b, 1, d) for n in range(6)]
        gn = norm_g[layer].reshape(4, 1, d)
        kind = layer % N_MIXERS
        j = counters[kind]
        counters[kind] += 1
        if kind == 0:
            qkv = _sb_in(x, gn[0], sc_m, sh_m, sb_w_in[j].astype(BF))
            o = _sb_attn(qkv)
            w_out = sb_w_out[j]
        elif kind == 1:
            w_in = _pad_cols(mla_w_in[j], 4 * LANES).astype(BF)
            w_q = mla_w_q_up[j].reshape(MLA_Q_LORA, MLA_HEADS, MLA_NOPE_DIM + MLA_ROPE_DIM)
            w_q = jnp.pad(w_q, ((0, 0), (0, 0), (0, MLA_QK_PAD - w_q.shape[-1])))
            w_q = w_q.reshape(MLA_Q_LORA, MLA_HEADS * MLA_QK_PAD).astype(BF)
            w_kv = mla_w_kv_up[j].reshape(MLA_KV_LORA, MLA_HEADS, MLA_NOPE_DIM + MLA_V_DIM)
            w_kv = jnp.concatenate([w_kv[..., :MLA_NOPE_DIM].reshape(MLA_KV_LORA, -1),
                                    w_kv[..., MLA_NOPE_DIM:].reshape(MLA_KV_LORA, -1)], axis=1)
            q, k, v = _mla_in(x, gn[0], sc_m, sh_m, w_in, mla_g_q[j].reshape(1, -1), w_q,
                              mla_g_kv[j].reshape(1, -1), w_kv.astype(BF), cos_m, sin_m)
            o = _flash(q, k, v, MLA_HEADS, MLA_QK_PAD)
            w_out = mla_w_out[j]
        else:
            w = dsa_w_in[j]
            n_main = DSA_NQ + 2 * DSA_NKV + DSA_NQI
            w = jnp.concatenate([w[:, :n_main], _pad_cols(w[:, n_main:n_main + IDX_DIM], LANES),
                                 _pad_cols(w[:, n_main + IDX_DIM:], LANES)], axis=1).astype(BF)
            gk = _pad_cols(dsa_g_kidx[j].reshape(1, -1), LANES)
            q, k, v, qi, ki, wi = _dsa_in(x, gn[0], sc_m, sh_m, w, gk, cos_a, sin_a, cos_i, sin_i)
            o = _dsa_attn(q, qi, wi, ki, k, v)
            w_out = dsa_w_out[j]
        x = _out_proj(o, w_out.astype(BF), x, gn[1], g_m)
        f = layer // 2
        if layer % 2 == 0:
            half = D_FF // 2
            w_gate, w_up = _split_gu(ffn_w_gu[f], D_FF)
            w_gate = w_gate.reshape(d, 2, half).transpose(1, 0, 2)
            w_up = w_up.reshape(d, 2, half).transpose(1, 0, 2)
            w_down = ffn_w_down[f].astype(BF).reshape(2, half, d)
            w_r = jnp.zeros((2, d, LANES), BF)
            b_r = jnp.zeros((1, LANES), F32)
            x = _ffn(x, gn[2], sc_f, sh_f, w_r, b_r, w_gate, w_up, w_down, gn[3], g_f, routed=False)
        else:
            w_gate, w_up = _split_gu(moe_w_gu[f], D_FF_EXPERT)
            w_r = _pad_cols(moe_w_router[f], LANES)
            w_r_hi = w_r.astype(BF)
            w_r = jnp.stack([w_r_hi, (w_r - w_r_hi.astype(F32)).astype(BF)])
            b_r = _pad_cols(moe_b_router[f].reshape(1, -1), LANES)
            x = _ffn(x, gn[2], sc_f, sh_f, w_r, b_r, w_gate, w_up, moe_w_down[f].astype(BF), gn[3],
                     g_f, routed=True)
    return x
```

```python
import functools

import jax
import jax.numpy as jnp
from jax import lax
from jax.experimental import pallas as pl
from jax.experimental.pallas import tpu as pltpu

BF = jnp.bfloat16
F32 = jnp.float32
I32 = jnp.int32

D_MODEL = 1024
N_MIXERS = 3
ROPE_THETA = 500000.0
EPS = 1e-6
LANES = 128

SB_HEADS = 8
SB_HEAD_DIM = 128
MLA_HEADS = 8
MLA_Q_LORA = 256
MLA_KV_LORA = 128
MLA_NOPE_DIM = 128
MLA_ROPE_DIM = 64
MLA_V_DIM = 128
MLA_QK_PAD = 256
DSA_HEADS = 8
DSA_KV_HEADS = 2
DSA_HEAD_DIM = 128
DSA_ROT_DIM = 32
IDX_HEADS = 8
IDX_DIM = 64
IDX_ROT_DIM = 16
DSA_TOPK_MAX = 256
D_FF = 2816
N_EXPERTS = 8
D_FF_EXPERT = 1408

VMEM_LIMIT = 56 * 1024 * 1024
INT_MIN = -(2 ** 31)
SB_UNDERFLOW = -104.0
NEG_BIG = -1e30


def _params(sem):
    return pltpu.CompilerParams(dimension_semantics=sem, vmem_limit_bytes=VMEM_LIMIT)


def _dot(a, b):
    return jnp.dot(a, b, preferred_element_type=F32)


def _dot_t(a, b):
    return lax.dot_general(a, b, (((1,), (1,)), ((), ())), preferred_element_type=F32)


def _rms(x, g):
    return x * lax.rsqrt(jnp.mean(x * x, axis=-1, keepdims=True) + EPS) * g


def _norm_mod(x, g, sc, sh):
    return _rms(x, g) * (1.0 + sc) + sh


def _rope(x, cos_t, sin_t, half, group):
    lane = lax.broadcasted_iota(I32, x.shape, 1) % group
    partner = jnp.where(lane < half, pltpu.roll(x, LANES - half, 1), pltpu.roll(x, half, 1))
    return x * cos_t + partner * sin_t


VT_ONES = 16
VT_ROWS = LANES + VT_ONES
M_INIT = -1e20


def _store_vt(vt_ref, head, v, stage_ref):
    r0 = head * VT_ROWS
    stage_ref[...] = v
    vt_ref[0, r0:r0 + LANES, :] = stage_ref[...].T.astype(BF)
    vt_ref[0, r0 + LANES:r0 + VT_ROWS, :] = jnp.ones((VT_ONES, v.shape[0]), BF)


def _softmax_step(s_t, m, acc, vt):
    m_new = jnp.maximum(m, jnp.max(s_t, axis=0, keepdims=True))
    p = jnp.exp(s_t - m_new).astype(BF)
    acc = jnp.exp(m - m_new) * acc + _dot(vt, p)
    return m_new, acc


def _softmax_finish(acc):
    return (acc[:LANES] / acc[LANES:LANES + 1]).T


def _ada_kernel(c_ref, w_ref, b_ref, o_ref):
    c = c_ref[...]
    cond = c / (1.0 + jnp.exp(-c))
    o_ref[0] = _dot(cond.astype(BF), w_ref[0].astype(BF)) + b_ref[0]


def _ada(c, ada_w, ada_b):
    depth, d, n = ada_w.shape
    rows = 8
    c_pad = jnp.zeros((rows, d), F32).at[: c.shape[0]].set(c)
    tn = 1536
    out = pl.pallas_call(
        _ada_kernel,
        name="ada_mod",
        out_shape=jax.ShapeDtypeStruct((depth, rows, n), F32),
        grid=(depth, n // tn),
        in_specs=[
            pl.BlockSpec((rows, d), lambda l, j: (0, 0)),
            pl.BlockSpec((1, d, tn), lambda l, j: (l, 0, j)),
            pl.BlockSpec((1, 1, tn), lambda l, j: (l, 0, j)),
        ],
        out_specs=pl.BlockSpec((1, rows, tn), lambda l, j: (l, 0, j)),
        compiler_params=_params(("parallel", "parallel")),
    )(c_pad, ada_w, ada_b.reshape(depth, 1, n))
    return out[:, : c.shape[0]]


def _out_kernel(o_ref, w_ref, x_ref, g_ref, gate_ref, xo_ref):
    y = _dot(o_ref[0], w_ref[...])
    xo_ref[0] = x_ref[0] + gate_ref[0] * _rms(y, g_ref[...])


def _out_proj(o, w, x, g, gate, tm=512):
    b, s, d = x.shape
    k = o.shape[-1]
    row = lambda bi, i: (bi, i, 0)
    return pl.pallas_call(
        _out_kernel,
        name="out_proj",
        out_shape=jax.ShapeDtypeStruct(x.shape, F32),
        grid=(b, s // tm),
        in_specs=[
            pl.BlockSpec((1, tm, k), row),
            pl.BlockSpec((k, d), lambda bi, i: (0, 0)),
            pl.BlockSpec((1, tm, d), row),
            pl.BlockSpec((1, d), lambda bi, i: (0, 0)),
            pl.BlockSpec((1, 1, d), lambda bi, i: (bi, 0, 0)),
        ],
        out_specs=pl.BlockSpec((1, tm, d), row),
        compiler_params=_params(("parallel", "parallel")),
    )(o, w, x, g, gate)


def _sb_in_kernel(x_ref, g_ref, sc_ref, sh_ref, w_ref, o_ref, *, tn, n_q):
    h = _norm_mod(x_ref[0], g_ref[...], sc_ref[0], sh_ref[0]).astype(BF)
    scale = SB_HEAD_DIM ** -0.5
    for j in range(0, w_ref.shape[1], tn):
        y = _dot(h, w_ref[:, j:j + tn])
        if j < n_q:
            y = y * scale
        o_ref[0, :, j:j + tn] = y.astype(BF)


def _sb_in(x, g, sc, sh, w, tm=512):
    b, s, d = x.shape
    n = w.shape[1]
    row = lambda bi, i: (bi, i, 0)
    vec = lambda bi, i: (bi, 0, 0)
    return pl.pallas_call(
        functools.partial(_sb_in_kernel, tn=512, n_q=SB_HEADS * SB_HEAD_DIM),
        name="sb_in",
        out_shape=jax.ShapeDtypeStruct((b, s, n), BF),
        grid=(b, s // tm),
        in_specs=[
            pl.BlockSpec((1, tm, d), row),
            pl.BlockSpec((1, d), lambda bi, i: (0, 0)),
            pl.BlockSpec((1, 1, d), vec),
            pl.BlockSpec((1, 1, d), vec),
            pl.BlockSpec((d, n), lambda bi, i: (0, 0)),
        ],
        out_specs=pl.BlockSpec((1, tm, n), row),
        compiler_params=_params(("parallel", "parallel")),
    )(x, g, sc, sh, w)


def _sb_attn_kernel(q_ref, k_ref, v_ref, o_ref, *, t, heads):
    i = pl.program_id(2)
    d = SB_HEAD_DIM
    r_io = lax.broadcasted_iota(I32, (t, t + LANES), 0)
    c_io = lax.broadcasted_iota(I32, (t, t + LANES), 1)
    after = jnp.logical_or(r_io > c_io, c_io >= t).astype(BF)
    mask = lax.broadcasted_iota(I32, (t, t), 1) < lax.broadcasted_iota(I32, (t, t), 0)

    def block(j, hd, run, acc, diagonal):
        start = pl.multiple_of(j * t, t)
        cols = slice(hd * d, (hd + 1) * d)
        z = _dot_t(q_ref[0, :, cols], k_ref[0, pl.ds(start, t), cols])
        sp = jnp.log1p(jnp.exp(-jnp.abs(z)))
        log_beta = jnp.minimum(z, 0.0) - sp
        log_fail = -jnp.maximum(z, 0.0) - sp
        if diagonal:
            log_fail = jnp.where(mask, log_fail, 0.0)
        hi = log_fail.astype(BF)
        lo = (log_fail - hi.astype(F32)).astype(BF)
        sums = _dot(hi, after) + _dot(lo, after)
        tail = jnp.concatenate(
            [sums[:, u * LANES:(u + 1) * LANES] + run for u in range(t // LANES)], axis=1)
        a = jnp.exp(log_beta + tail)
        if diagonal:
            a = jnp.where(mask, a, 0.0)
        acc = acc + _dot(a.astype(BF), v_ref[0, pl.ds(start, t), cols])
        return run + sums[:, t:], acc

    def blocks(j, state, diagonal):
        out = []
        for hd in range(heads):
            out.extend(block(j, hd, state[2 * hd], state[2 * hd + 1], diagonal))
        return tuple(out)

    def live(state):
        top = state[0]
        for hd in range(1, heads):
            top = jnp.maximum(top, state[2 * hd])
        return jnp.max(top) > SB_UNDERFLOW

    state = blocks(i, (jnp.zeros((t, LANES), F32), jnp.zeros((t, d), F32)) * heads, True)

    def cond(carry):
        return jnp.logical_and(carry[0] >= 0, carry[1])

    def body(carry):
        state = blocks(carry[0], carry[2:], False)
        return (carry[0] - 1, live(state)) + state

    res = lax.while_loop(cond, body, (i - 1, live(state)) + state)
    for hd in range(heads):
        o_ref[0, :, hd * d:(hd + 1) * d] = res[2 + 2 * hd + 1].astype(BF)


def _sb_attn(qkv, t=256, heads=2):
    b, s, _ = qkv.shape
    ng = SB_HEADS // heads
    w = heads * SB_HEAD_DIM
    return pl.pallas_call(
        functools.partial(_sb_attn_kernel, t=t, heads=heads),
        name="sb_attn",
        out_shape=jax.ShapeDtypeStruct((b, s, SB_HEADS * SB_HEAD_DIM), BF),
        grid=(b, ng, s // t),
        in_specs=[
            pl.BlockSpec((1, t, w), lambda bi, h, i: (bi, i, h)),
            pl.BlockSpec((1, s, w), lambda bi, h, i: (bi, 0, ng + h)),
            pl.BlockSpec((1, s, w), lambda bi, h, i: (bi, 0, 2 * ng + h)),
        ],
        out_specs=pl.BlockSpec((1, t, w), lambda bi, h, i: (bi, i, h)),
        compiler_params=_params(("parallel", "parallel", "arbitrary")),
    )(qkv, qkv, qkv)


def _mla_in_kernel(x_ref, g_ref, sc_ref, sh_ref, w_in_ref, gq_ref, wq_ref, gkv_ref, wkv_ref,
                   cos_ref, sin_ref, q_ref, k_ref, v_ref, stage_ref):
    h = _norm_mod(x_ref[0], g_ref[...], sc_ref[0], sh_ref[0]).astype(BF)
    lat = _dot(h, w_in_ref[...])
    cos_t, sin_t = cos_ref[0], sin_ref[0]
    half = MLA_ROPE_DIM // 2
    c_q = _rms(lat[:, :MLA_Q_LORA], gq_ref[...]).astype(BF)
    c_kv = _rms(lat[:, MLA_Q_LORA:MLA_Q_LORA + MLA_KV_LORA], gkv_ref[...]).astype(BF)
    k_rope = _rope(lat[:, MLA_Q_LORA + MLA_KV_LORA:], cos_t, sin_t, half, LANES).astype(BF)
    scale = (MLA_NOPE_DIM + MLA_ROPE_DIM) ** -0.5
    for hd in range(MLA_HEADS):
        c0 = hd * MLA_QK_PAD
        qh = _dot(c_q, wq_ref[:, c0:c0 + MLA_QK_PAD])
        q_ref[0, :, c0:c0 + LANES] = (qh[:, :LANES] * scale).astype(BF)
        q_ref[0, :, c0 + LANES:c0 + MLA_QK_PAD] = (
            _rope(qh[:, LANES:], cos_t, sin_t, half, LANES) * scale).astype(BF)
        k_ref[0, :, c0:c0 + LANES] = _dot(c_kv, wkv_ref[:, hd * LANES:(hd + 1) * LANES]).astype(BF)
        k_ref[0, :, c0 + LANES:c0 + MLA_QK_PAD] = k_rope
    nk = MLA_HEADS * MLA_NOPE_DIM
    for hd in range(MLA_HEADS):
        v = _dot(c_kv, wkv_ref[:, nk + hd * LANES:nk + (hd + 1) * LANES])
        _store_vt(v_ref, hd, v, stage_ref)


def _mla_in(x, g, sc, sh, w_in, g_q, w_q, g_kv, w_kv, cos_t, sin_t, tm=512):
    b, s, d = x.shape
    row = lambda bi, i: (bi, i, 0)
    vec = lambda bi, i: (bi, 0, 0)
    full = lambda bi, i: (0, 0)
    nqk = MLA_HEADS * MLA_QK_PAD
    nv = MLA_HEADS * VT_ROWS
    return pl.pallas_call(
        _mla_in_kernel,
        name="mla_in",
        out_shape=(jax.ShapeDtypeStruct((b, s, nqk), BF), jax.ShapeDtypeStruct((b, s, nqk), BF),
                   jax.ShapeDtypeStruct((b, nv, s), BF)),
        grid=(b, s // tm),
        in_specs=[
            pl.BlockSpec((1, tm, d), row),
            pl.BlockSpec((1, d), full),
            pl.BlockSpec((1, 1, d), vec),
            pl.BlockSpec((1, 1, d), vec),
            pl.BlockSpec(w_in.shape, full),
            pl.BlockSpec(g_q.shape, full),
            pl.BlockSpec(w_q.shape, full),
            pl.BlockSpec(g_kv.shape, full),
            pl.BlockSpec(w_kv.shape, full),
            pl.BlockSpec((1, tm, LANES), row),
            pl.BlockSpec((1, tm, LANES), row),
        ],
        out_specs=(pl.BlockSpec((1, tm, nqk), row), pl.BlockSpec((1, tm, nqk), row),
                   pl.BlockSpec((1, nv, tm), lambda bi, i: (bi, 0, i))),
        scratch_shapes=[pltpu.VMEM((tm, LANES), F32)],
        compiler_params=_params(("parallel", "parallel")),
    )(x, g, sc, sh, w_in, g_q, w_q, g_kv, w_kv, cos_t, sin_t)


def _flash_kernel(q_ref, k_ref, vt_ref, o_ref, *, t):
    i = pl.program_id(2)
    q = q_ref[0]

    def scores(j):
        start = pl.multiple_of(j * t, t)
        return _dot_t(k_ref[0, pl.ds(start, t), :], q)

    def step(j, s_t, m, acc):
        start = pl.multiple_of(j * t, t)
        return _softmax_step(s_t, m, acc, vt_ref[0, :, pl.ds(start, t)])

    def pair(jj, carry):
        s_a, s_b = scores(2 * jj), scores(2 * jj + 1)
        return step(2 * jj + 1, s_b, *step(2 * jj, s_a, *carry))

    init = (jnp.full((1, t), M_INIT, F32), jnp.zeros((VT_ROWS, t), F32))
    m, acc = lax.fori_loop(0, i // 2, pair, init)
    key_io = lax.broadcasted_iota(I32, (t, t), 0)
    qry_io = lax.broadcasted_iota(I32, (t, t), 1)
    s_diag = jnp.where(key_io <= qry_io, scores(i), NEG_BIG)
    m, acc = lax.cond(i % 2 == 1, lambda c: step(i - 1, scores(i - 1), *c), lambda c: c, (m, acc))
    m, acc = step(i, s_diag, m, acc)
    o_ref[0] = _softmax_finish(acc).astype(BF)


def _flash(q, k, vt, heads, dk, t=512):
    b, s, _ = q.shape
    return pl.pallas_call(
        functools.partial(_flash_kernel, t=t),
        name="flash_attn",
        out_shape=jax.ShapeDtypeStruct((b, s, heads * LANES), BF),
        grid=(b, heads, s // t),
        in_specs=[
            pl.BlockSpec((1, t, dk), lambda bi, h, i: (bi, i, h)),
            pl.BlockSpec((1, s, dk), lambda bi, h, i: (bi, 0, h)),
            pl.BlockSpec((1, VT_ROWS, s), lambda bi, h, i: (bi, h, 0)),
        ],
        out_specs=pl.BlockSpec((1, t, LANES), lambda bi, h, i: (bi, i, h)),
        compiler_params=_params(("parallel", "parallel", "arbitrary")),
    )(q, k, vt)


DSA_NQ = DSA_HEADS * DSA_HEAD_DIM
DSA_NKV = DSA_KV_HEADS * DSA_HEAD_DIM
DSA_NQI = IDX_HEADS * IDX_DIM
DSA_COLS = DSA_NQ + 2 * DSA_NKV + DSA_NQI + 2 * LANES


def _dsa_in_kernel(x_ref, g_ref, sc_ref, sh_ref, w_ref, gk_ref, cos_a_ref, sin_a_ref, cos_i_ref,
                   sin_i_ref, q_ref, k_ref, v_ref, qi_ref, ki_ref, wi_ref, stage_ref):
    h = _norm_mod(x_ref[0], g_ref[...], sc_ref[0], sh_ref[0]).astype(BF)
    cos_a, sin_a = cos_a_ref[0], sin_a_ref[0]
    cos_i, sin_i = cos_i_ref[0], sin_i_ref[0]
    scale = DSA_HEAD_DIM ** -0.5
    ha, hi = DSA_ROT_DIM // 2, IDX_ROT_DIM // 2
    for hd in range(DSA_HEADS):
        c0 = hd * LANES
        y = _dot(h, w_ref[:, c0:c0 + LANES])
        q_ref[0, :, c0:c0 + LANES] = (_rope(y, cos_a, sin_a, ha, LANES) * scale).astype(BF)
    for hd in range(DSA_KV_HEADS):
        c0 = hd * LANES
        y = _dot(h, w_ref[:, DSA_NQ + c0:DSA_NQ + c0 + LANES])
        k_ref[0, :, c0:c0 + LANES] = _rope(y, cos_a, sin_a, ha, LANES).astype(BF)
    off = DSA_NQ + DSA_NKV
    for hd in range(DSA_KV_HEADS):
        _store_vt(v_ref, hd, _dot(h, w_ref[:, off + hd * LANES:off + (hd + 1) * LANES]), stage_ref)
    off += DSA_NKV
    for p in range(DSA_NQI // LANES):
        c0 = p * LANES
        y = _dot(h, w_ref[:, off + c0:off + c0 + LANES])
        qi_ref[0, :, c0:c0 + LANES] = _rope(y, cos_i, sin_i, hi, IDX_DIM).astype(BF)
    off += DSA_NQI
    y = _dot(h, w_ref[:, off:off + LANES])
    y = y * lax.rsqrt(jnp.sum(y * y, axis=-1, keepdims=True) * (1.0 / IDX_DIM) + EPS) * gk_ref[...]
    y = _rope(y, cos_i, sin_i, hi, IDX_DIM)
    ki_ref[0] = (y + pltpu.roll(y, IDX_DIM, 1)).astype(BF)
    off += LANES
    wi_ref[0] = _dot(h, w_ref[:, off:off + LANES]) * (IDX_HEADS ** -0.5 * IDX_DIM ** -0.5)


def _dsa_in(x, g, sc, sh, w, gk, cos_a, sin_a, cos_i, sin_i, tm=512):
    b, s, d = x.shape
    row = lambda bi, i: (bi, i, 0)
    vec = lambda bi, i: (bi, 0, 0)
    full = lambda bi, i: (0, 0)
    tab = pl.BlockSpec((1, tm, LANES), row)
    nvt = DSA_KV_HEADS * VT_ROWS

    def spec(n):
        return pl.BlockSpec((1, tm, n), row)

    def sds(n, dt=BF):
        return jax.ShapeDtypeStruct((b, s, n), dt)

    return pl.pallas_call(
        _dsa_in_kernel,
        name="dsa_in",
        out_shape=(sds(DSA_NQ), sds(DSA_NKV), jax.ShapeDtypeStruct((b, nvt, s), BF), sds(DSA_NQI),
                   sds(LANES), sds(LANES, F32)),
        grid=(b, s // tm),
        in_specs=[
            pl.BlockSpec((1, tm, d), row),
            pl.BlockSpec((1, d), full),
            pl.BlockSpec((1, 1, d), vec),
            pl.BlockSpec((1, 1, d), vec),
            pl.BlockSpec(w.shape, full),
            pl.BlockSpec(gk.shape, full),
            tab, tab, tab, tab,
        ],
        out_specs=(spec(DSA_NQ), spec(DSA_NKV), pl.BlockSpec((1, nvt, tm), lambda bi, i: (bi, 0, i)),
                   spec(DSA_NQI), spec(LANES), spec(LANES)),
        scratch_shapes=[pltpu.VMEM((tm, LANES), F32)],
        compiler_params=_params(("parallel", "parallel")),
    )(x, g, sc, sh, w, gk, cos_a, sin_a, cos_i, sin_i)


def _dsa_attn_kernel(q_ref, qi_ref, wi_ref, ki_ref, k_ref, vt_ref, o_ref, key_scr, *, tq, tk, n_sel):
    i = pl.program_id(1)
    n_chunks = (i * tq + tq + tk - 1) // tk
    group = DSA_HEADS // DSA_KV_HEADS
    key_io = lax.broadcasted_iota(I32, (tk, tq), 0)
    qry = lax.broadcasted_iota(I32, (1, tq), 1) + i * tq
    lane = lax.broadcasted_iota(I32, (tq, LANES), 1)

    qi = qi_ref[0]
    zero = jnp.zeros_like(qi[:, :LANES])
    stacked = []
    for hd in range(IDX_HEADS):
        pair = qi[:, (hd // 2) * LANES:(hd // 2 + 1) * LANES]
        low = (hd % 2) == 0
        stacked.append(jnp.where((lane < IDX_DIM) == low, pair, zero))
    q_all = jnp.concatenate(stacked, axis=0)
    w_t = wi_ref[0].T

    def score_chunk(c, _):
        start = pl.multiple_of(c * tk, tk)
        logits = _dot_t(ki_ref[0, pl.ds(start, tk), :], q_all)
        score = jnp.zeros((tk, tq), F32)
        for hd in range(IDX_HEADS):
            score = score + jnp.maximum(logits[:, hd * tq:(hd + 1) * tq], 0.0) * w_t[hd:hd + 1]
        bits = pltpu.bitcast(score, I32)
        key = jnp.where(bits < 0, bits ^ 0x7FFFFFFF, bits)
        key = jnp.where(score == 0.0, 0, key)
        key = jnp.where(key_io + start <= qry, key, INT_MIN)
        key_scr[pl.ds(start, tk), :] = key
        return 0

    lax.fori_loop(0, n_chunks, score_chunk, 0)

    tc = min(tk, 512)

    def count(pred):
        def chunk(c, cnt):
            start = pl.multiple_of(c * tc, tc)
            key = key_scr[pl.ds(start, tc), :].reshape(tc // 8, 8, tq)
            krow = (lax.broadcasted_iota(I32, (tc, tq), 0) + start).reshape(tc // 8, 8, tq)
            return cnt + jnp.sum(pred(key, krow).astype(I32), axis=0)
        cnt = lax.fori_loop(0, n_chunks * (tk // tc), chunk, jnp.zeros((8, tq), I32))
        for shift in (4, 2, 1):
            cnt = cnt + pltpu.roll(cnt, shift, 0)
        return cnt

    bits_per_check = 4

    def search_cond(carry):
        b, _, _, done = carry
        return jnp.logical_and(b < 32, jnp.min(done) == 0)

    def search_bits(carry):
        b, thr, n_at, done = carry
        for u in range(bits_per_check):
            cand = thr + lax.shift_left(jnp.int32(1), 31 - (b + u))
            n_ge = count(lambda key, krow: key >= cand)
            take = jnp.logical_and(n_ge >= n_sel, done == 0)
            thr = jnp.where(take, cand, thr)
            n_at = jnp.where(take, n_ge, n_at)
            done = jnp.where(n_at == n_sel, 1, done)
        return b + bits_per_check, thr, n_at, done

    thr0 = jnp.full((8, tq), INT_MIN, I32)
    n_at0 = jnp.zeros((8, tq), I32) + n_chunks * tk
    done0 = jnp.where(jnp.logical_or(qry < n_sel, n_at0 == n_sel), 1, 0)
    _, thr, n_at, _ = lax.while_loop(search_cond, search_bits, (jnp.int32(0), thr0, n_at0, done0))
    tie_bits = key_scr.shape[0].bit_length()

    def tie_search(_):
        want = n_sel - count(lambda key, krow: key > thr)

        def cut_bit(b, cut):
            cand = cut + lax.shift_left(jnp.int32(1), tie_bits - 1 - b)
            n_tie = count(lambda key, krow: jnp.logical_and(key == thr, krow < cand))
            return jnp.where(n_tie <= want, cand, cut)
        return lax.fori_loop(0, tie_bits, cut_bit, jnp.zeros((8, tq), I32))

    crowded = jnp.max(jnp.where(jnp.logical_and(n_at > n_sel, thr > INT_MIN), 1, 0)) > 0
    cut = lax.cond(crowded, tie_search, lambda _: jnp.full((8, tq), 2 ** tie_bits - 1, I32), 0)
    thr, cut = thr[:1], cut[:1]

    q = q_ref[0]
    q_g = [jnp.concatenate([q[:, (g * group + n) * LANES:(g * group + n + 1) * LANES]
                            for n in range(group)], axis=0) for g in range(DSA_KV_HEADS)]

    def scores(c, g):
        start = pl.multiple_of(c * tk, tk)
        return _dot_t(k_ref[0, pl.ds(start, tk), g * LANES:(g + 1) * LANES], q_g[g])

    def attn_chunk(c, carry):
        start = pl.multiple_of(c * tk, tk)
        s_t = [scores(c, g) for g in range(DSA_KV_HEADS)]
        key = key_scr[pl.ds(start, tk), :]
        krow = key_io + start
        sel = jnp.logical_or(key > thr, jnp.logical_and(key == thr, krow < cut))
        sel = jnp.logical_and(sel, krow <= qry)
        bias = jnp.where(sel, 0.0, NEG_BIG)
        bias = jnp.concatenate([bias] * group, axis=1)
        out = []
        for g in range(DSA_KV_HEADS):
            m, acc = carry[2 * g:2 * g + 2]
            out.extend(_softmax_step(
                s_t[g] + bias, m, acc, vt_ref[0, g * VT_ROWS:(g + 1) * VT_ROWS, pl.ds(start, tk)]))
        return tuple(out)

    init = (jnp.full((1, group * tq), M_INIT, F32), jnp.zeros((VT_ROWS, group * tq), F32))
    res = lax.fori_loop(0, n_chunks, attn_chunk, init * DSA_KV_HEADS)
    for g in range(DSA_KV_HEADS):
        acc = res[2 * g + 1]
        for n in range(group):
            hd = g * group + n
            o_ref[0, :, hd * LANES:(hd + 1) * LANES] = _softmax_finish(
                acc[:, n * tq:(n + 1) * tq]).astype(BF)


def _dsa_attn(q, qi, wi, ki, k, vt, tq=128, tk=1024):
    b, s, _ = q.shape
    n_sel = min(DSA_TOPK_MAX, s // 4)
    tk = min(tk, s)
    row = lambda bi, i: (bi, i, 0)
    whole = lambda bi, i: (bi, 0, 0)
    return pl.pallas_call(
        functools.partial(_dsa_attn_kernel, tq=tq, tk=tk, n_sel=n_sel),
        name="dsa_attn",
        out_shape=jax.ShapeDtypeStruct((b, s, DSA_NQ), BF),
        grid=(b, s // tq),
        in_specs=[
            pl.BlockSpec((1, tq, DSA_NQ), row),
            pl.BlockSpec((1, tq, DSA_NQI), row),
            pl.BlockSpec((1, tq, LANES), row),
            pl.BlockSpec((1, s, LANES), whole),
            pl.BlockSpec((1, s, DSA_NKV), whole),
            pl.BlockSpec((1, DSA_KV_HEADS * VT_ROWS, s), whole),
        ],
        out_specs=pl.BlockSpec((1, tq, DSA_NQ), row),
        scratch_shapes=[pltpu.VMEM((s, tq), I32)],
        compiler_params=_params(("parallel", "arbitrary")),
    )(q, qi, wi, ki, k, vt)


def _ffn_kernel(x_ref, g_ref, sc_ref, sh_ref, wr_ref, br_ref, wg_ref, wu_ref, wd_ref, gp_ref,
                gate_ref, xo_ref, h_scr, gates_scr, acc_scr, *, routed):
    e = pl.program_id(2)
    lane = lax.broadcasted_iota(I32, gates_scr.shape, 1)

    @pl.when(e == 0)
    def _():
        h32 = _norm_mod(x_ref[0], g_ref[...], sc_ref[0], sh_ref[0])
        h = h32.astype(BF)
        h_scr[...] = h
        acc_scr[...] = jnp.zeros(acc_scr.shape, F32)
        if routed:
            h_lo = (h32 - h.astype(F32)).astype(BF)
            logits = _dot(h, wr_ref[0]) + (_dot(h, wr_ref[1]) + _dot(h_lo, wr_ref[0]))
            logits = jnp.where(lane < N_EXPERTS, logits + br_ref[...], -jnp.inf)
            m1 = jnp.max(logits, axis=-1, keepdims=True)
            i1 = jnp.min(jnp.where(logits == m1, lane, LANES), axis=-1, keepdims=True)
            rest = jnp.where(lane == i1, -jnp.inf, logits)
            m2 = jnp.max(rest, axis=-1, keepdims=True)
            i2 = jnp.min(jnp.where(rest == m2, lane, LANES), axis=-1, keepdims=True)
            e2 = jnp.exp(m2 - m1)
            w1 = 1.0 / (1.0 + e2)
            gates_scr[...] = jnp.where(lane == i1, w1, 0.0) + jnp.where(lane == i2, e2 * w1, 0.0)

    h = h_scr[...]
    a = _dot(h, wg_ref[0])
    u = _dot(h, wu_ref[0])
    act = (a / (1.0 + jnp.exp(-a)) * u).astype(BF)
    y = _dot(act, wd_ref[0])
    if routed:
        y = y * jnp.sum(jnp.where(lane == e, gates_scr[...], 0.0), axis=-1, keepdims=True)
    acc_scr[...] += y

    @pl.when(e == pl.num_programs(2) - 1)
    def _():
        xo_ref[0] = x_ref[0] + gate_ref[0] * _rms(acc_scr[...], gp_ref[...])


def _ffn(x, g, sc, sh, w_router, b_router, w_gate, w_up, w_down, g_post, gate, routed, tm=512):
    b, s, d = x.shape
    n_e, _, f = w_gate.shape
    row = lambda bi, i, e: (bi, i, 0)
    vec = lambda bi, i, e: (bi, 0, 0)
    full = lambda bi, i, e: (0, 0)
    slab = lambda bi, i, e: (e, 0, 0)
    return pl.pallas_call(
        functools.partial(_ffn_kernel, routed=routed),
        name="moe" if routed else "ffn",
        out_shape=jax.ShapeDtypeStruct(x.shape, F32),
        grid=(b, s // tm, n_e),
        in_specs=[
            pl.BlockSpec((1, tm, d), row),
            pl.BlockSpec((1, d), full),
            pl.BlockSpec((1, 1, d), vec),
            pl.BlockSpec((1, 1, d), vec),
            pl.BlockSpec((2, d, LANES), lambda bi, i, e: (0, 0, 0)),
            pl.BlockSpec((1, LANES), full),
            pl.BlockSpec((1, d, f), slab),
            pl.BlockSpec((1, d, f), slab),
            pl.BlockSpec((1, f, d), slab),
            pl.BlockSpec((1, d), full),
            pl.BlockSpec((1, 1, d), vec),
        ],
        out_specs=pl.BlockSpec((1, tm, d), row),
        scratch_shapes=[
            pltpu.VMEM((tm, d), BF),
            pltpu.VMEM((tm, LANES), F32),
            pltpu.VMEM((tm, d), F32),
        ],
        compiler_params=_params(("parallel", "parallel", "arbitrary")),
    )(x, g, sc, sh, w_router, b_router, w_gate, w_up, w_down, g_post, gate)


MOE_TILE = 1024
MOE_ROWS = 384


def _moe_kernel(x_ref, g_ref, sc_ref, sh_ref, wr_ref, br_ref, wg_ref, wu_ref, wd_ref, gp_ref,
                gate_ref, xo_ref, h_scr, gate_c, rank_c, rank_r, acc_scr, *, rows):
    e = pl.program_id(2)
    t = h_scr.shape[0]
    lane = lax.broadcasted_iota(I32, (t, LANES), 1)

    @pl.when(e == 0)
    def _():
        h32 = _norm_mod(x_ref[0], g_ref[...], sc_ref[0], sh_ref[0])
        h = h32.astype(BF)
        h_scr[...] = h
        acc_scr[...] = jnp.zeros(acc_scr.shape, F32)
        h_lo = (h32 - h.astype(F32)).astype(BF)
        logits = _dot(h, wr_ref[0]) + (_dot(h, wr_ref[1]) + _dot(h_lo, wr_ref[0]))
        logits = jnp.where(lane < N_EXPERTS, logits + br_ref[...], -jnp.inf)
        m1 = jnp.max(logits, axis=-1, keepdims=True)
        i1 = jnp.min(jnp.where(logits == m1, lane, LANES), axis=-1, keepdims=True)
        rest = jnp.where(lane == i1, -jnp.inf, logits)
        m2 = jnp.max(rest, axis=-1, keepdims=True)
        i2 = jnp.min(jnp.where(rest == m2, lane, LANES), axis=-1, keepdims=True)
        e2 = jnp.exp(m2 - m1)
        w1 = 1.0 / (1.0 + e2)
        gate_c[...] = jnp.where(lane == i1, w1, 0.0) + jnp.where(lane == i2, e2 * w1, 0.0)
        sel = jnp.logical_or(lane == i1, lane == i2)
        earlier = (lax.broadcasted_iota(I32, (t, t), 1) < lax.broadcasted_iota(I32, (t, t), 0))
        rank = _dot(earlier.astype(BF), jnp.where(sel, 1.0, 0.0).astype(BF))
        rank = jnp.where(sel, rank, -1.0)
        rank_c[...] = rank
        rank_r[...] = rank.T

    rank_row = rank_r[pl.ds(e, 1), :]
    on_e = lane == e
    rank_col = jnp.sum(jnp.where(on_e, rank_c[...], 0.0), axis=-1, keepdims=True)
    gate_col = jnp.sum(jnp.where(on_e, gate_c[...], 0.0), axis=-1, keepdims=True)
    n_tokens = jnp.max(rank_row).astype(I32) + 1

    def expert_pass(p, _):
        base = (p * rows).astype(F32)
        row_io = lax.broadcasted_iota(I32, (rows, t), 0).astype(F32) + base
        pick = jnp.where(rank_row == row_io, 1.0, 0.0).astype(BF)
        hs = _dot(pick, h_scr[...]).astype(BF)
        a = _dot(hs, wg_ref[0])
        u = _dot(hs, wu_ref[0])
        act = (a / (1.0 + jnp.exp(-a)) * u).astype(BF)
        y = _dot(act, wd_ref[0]).astype(BF)
        col_io = lax.broadcasted_iota(I32, (t, rows), 1).astype(F32) + base
        place = jnp.where(rank_col == col_io, 1.0, 0.0).astype(BF)
        acc_scr[...] += gate_col * _dot(place, y)
        return 0

    lax.fori_loop(0, (n_tokens + rows - 1) // rows, expert_pass, 0)

    @pl.when(e == pl.num_programs(2) - 1)
    def _():
        xo_ref[0] = x_ref[0] + gate_ref[0] * _rms(acc_scr[...], gp_ref[...])


def _moe(x, g, sc, sh, w_router, b_router, w_gate, w_up, w_down, g_post, gate):
    b, s, d = x.shape
    n_e, _, f = w_gate.shape
    tm = min(MOE_TILE, s)
    row = lambda bi, i, e: (bi, i, 0)
    vec = lambda bi, i, e: (bi, 0, 0)
    full = lambda bi, i, e: (0, 0)
    slab = lambda bi, i, e: (e, 0, 0)
    once = pl.Buffered(1)
    return pl.pallas_call(
        functools.partial(_moe_kernel, rows=min(MOE_ROWS, tm)),
        name="moe",
        out_shape=jax.ShapeDtypeStruct(x.shape, F32),
        grid=(b, s // tm, n_e),
        in_specs=[
            pl.BlockSpec((1, tm, d), row, pipeline_mode=once),
            pl.BlockSpec((1, d), full),
            pl.BlockSpec((1, 1, d), vec),
            pl.BlockSpec((1, 1, d), vec),
            pl.BlockSpec((2, d, LANES), lambda bi, i, e: (0, 0, 0)),
            pl.BlockSpec((1, LANES), full),
            pl.BlockSpec((1, d, f), slab),
            pl.BlockSpec((1, d, f), slab),
            pl.BlockSpec((1, f, d), slab),
            pl.BlockSpec((1, d), full),
            pl.BlockSpec((1, 1, d), vec),
        ],
        out_specs=pl.BlockSpec((1, tm, d), row, pipeline_mode=once),
        scratch_shapes=[
            pltpu.VMEM((tm, d), BF),
            pltpu.VMEM((tm, LANES), F32),
            pltpu.VMEM((tm, LANES), F32),
            pltpu.VMEM((LANES, tm), F32),
            pltpu.VMEM((tm, d), F32),
        ],
        compiler_params=_params(("parallel", "parallel", "arbitrary")),
    )(x, g, sc, sh, w_router, b_router, w_gate, w_up, w_down, g_post, gate)


def _rope_tables(positions, rot_dim, group):
    half = rot_dim // 2
    inv_freq = ROPE_THETA ** (-jnp.arange(0, rot_dim, 2, dtype=F32) / rot_dim)
    ang = positions.astype(F32)[..., None] * inv_freq
    cos, sin = jnp.cos(ang), jnp.sin(ang)
    pad = group - rot_dim
    ones = jnp.ones(cos.shape[:-1] + (pad,), F32)
    zeros = jnp.zeros(cos.shape[:-1] + (pad,), F32)
    cos_t = jnp.concatenate([cos, cos, ones], axis=-1)
    sin_t = jnp.concatenate([-sin, sin, zeros], axis=-1)
    reps = LANES // group
    return jnp.tile(cos_t, (1, 1, reps)), jnp.tile(sin_t, (1, 1, reps))


def _pad_cols(w, n):
    return jnp.pad(w, ((0, 0), (0, n - w.shape[1])))


def _split_gu(w_gu, f):
    return w_gu[..., :f].astype(BF), w_gu[..., f:].astype(BF)


def kernel(x, c, positions, ada_w, ada_b, norm_g, sb_w_in, sb_w_out, mla_w_in, mla_g_q, mla_w_q_up,
           mla_g_kv, mla_w_kv_up, mla_w_out, dsa_w_in, dsa_g_kidx, dsa_w_out, ffn_w_gu, ffn_w_down,
           moe_w_router, moe_b_router, moe_w_gu, moe_w_down):
    depth = ada_w.shape[0]
    b, s, d = x.shape
    mod = _ada(c, ada_w, ada_b)
    cos_m, sin_m = _rope_tables(positions, MLA_ROPE_DIM, LANES)
    cos_a, sin_a = _rope_tables(positions, DSA_ROT_DIM, LANES)
    cos_i, sin_i = _rope_tables(positions, IDX_ROT_DIM, IDX_DIM)
    counters = [0, 0, 0]
    for layer in range(depth):
        sh_m, sc_m, g_m, sh_f, sc_f, g_f = [
            mod[layer, :, n * d:(n + 1) * d].reshape(b, 1, d) for n in range(6)]
        gn = norm_g[layer].reshape(4, 1, d)
        kind = layer % N_MIXERS
        j = counters[kind]
        counters[kind] += 1
        if kind == 0:
            qkv = _sb_in(x, gn[0], sc_m, sh_m, sb_w_in[j].astype(BF))
            o = _sb_attn(qkv)
            w_out = sb_w_out[j]
        elif kind == 1:
            w_in = _pad_cols(mla_w_in[j], 4 * LANES).astype(BF)
            w_q = mla_w_q_up[j].reshape(MLA_Q_LORA, MLA_HEADS, MLA_NOPE_DIM + MLA_ROPE_DIM)
            w_q = jnp.pad(w_q, ((0, 0), (0, 0), (0, MLA_QK_PAD - w_q.shape[-1])))
            w_q = w_q.reshape(MLA_Q_LORA, MLA_HEADS * MLA_QK_PAD).astype(BF)
            w_kv = mla_w_kv_up[j].reshape(MLA_KV_LORA, MLA_HEADS, MLA_NOPE_DIM + MLA_V_DIM)
            w_kv = jnp.concatenate([w_kv[..., :MLA_NOPE_DIM].reshape(MLA_KV_LORA, -1),
                                    w_kv[..., MLA_NOPE_DIM:].reshape(MLA_KV_LORA, -1)], axis=1)
            q, k, v = _mla_in(x, gn[0], sc_m, sh_m, w_in, mla_g_q[j].reshape(1, -1), w_q,
                              mla_g_kv[j].reshape(1, -1), w_kv.astype(BF), cos_m, sin_m)
            o = _flash(q, k, v, MLA_HEADS, MLA_QK_PAD)
            w_out = mla_w_out[j]
        else:
            w = dsa_w_in[j]
            n_main = DSA_NQ + 2 * DSA_NKV + DSA_NQI
            w = jnp.concatenate([w[:, :n_main], _pad_cols(w[:, n_main:n_main + IDX_DIM], LANES),
                                 _pad_cols(w[:, n_main + IDX_DIM:], LANES)], axis=1).astype(BF)
            gk = _pad_cols(dsa_g_kidx[j].reshape(1, -1), LANES)
            q, k, v, qi, ki, wi = _dsa_in(x, gn[0], sc_m, sh_m, w, gk, cos_a, sin_a, cos_i, sin_i)
            o = _dsa_attn(q, qi, wi, ki, k, v)
            w_out = dsa_w_out[j]
        x = _out_proj(o, w_out.astype(BF), x, gn[1], g_m)
        f = layer // 2
        if layer % 2 == 0:
            half = D_FF // 2
            w_gate, w_up = _split_gu(ffn_w_gu[f], D_FF)
            w_gate = w_gate.reshape(d, 2, half).transpose(1, 0, 2)
            w_up = w_up.reshape(d, 2, half).transpose(1, 0, 2)
            w_down = ffn_w_down[f].astype(BF).reshape(2, half, d)
            w_r = jnp.zeros((2, d, LANES), BF)
            b_r = jnp.zeros((1, LANES), F32)
            x = _ffn(x, gn[2], sc_f, sh_f, w_r, b_r, w_gate, w_up, w_down, gn[3], g_f, routed=False)
        else:
            w_gate, w_up = _split_gu(moe_w_gu[f], D_FF_EXPERT)
            w_r = _pad_cols(moe_w_router[f], LANES)
            w_r_hi = w_r.astype(BF)
            w_r = jnp.stack([w_r_hi, (w_r - w_r_hi.astype(F32)).astype(BF)])
            b_r = _pad_cols(moe_b_router[f].reshape(1, -1), LANES)
            x = _moe(x, gn[2], sc_f, sh_f, w_r, b_r, w_gate, w_up, moe_w_down[f].astype(BF), gn[3],
                     g_f)
    return x
```

```python
import functools

import jax
import jax.numpy as jnp
from jax import lax
from jax.experimental import pallas as pl
from jax.experimental.pallas import tpu as pltpu

BF = jnp.bfloat16
F32 = jnp.float32
I32 = jnp.int32

D_MODEL = 1024
N_MIXERS = 3
ROPE_THETA = 500000.0
EPS = 1e-6
LANES = 128

SB_HEADS = 8
SB_HEAD_DIM = 128
MLA_HEADS = 8
MLA_Q_LORA = 256
MLA_KV_LORA = 128
MLA_NOPE_DIM = 128
MLA_ROPE_DIM = 64
MLA_V_DIM = 128
MLA_QK_PAD = 256
DSA_HEADS = 8
DSA_KV_HEADS = 2
DSA_HEAD_DIM = 128
DSA_ROT_DIM = 32
IDX_HEADS = 8
IDX_DIM = 64
IDX_ROT_DIM = 16
DSA_TOPK_MAX = 256
D_FF = 2816
N_EXPERTS = 8
D_FF_EXPERT = 1408

VMEM_LIMIT = 56 * 1024 * 1024
INT_MIN = -(2 ** 31)
SB_UNDERFLOW = -104.0
NEG_BIG = -1e30


def _params(sem):
    return pltpu.CompilerParams(dimension_semantics=sem, vmem_limit_bytes=VMEM_LIMIT)


def _dot(a, b):
    return jnp.dot(a, b, preferred_element_type=F32)


def _dot_t(a, b):
    return lax.dot_general(a, b, (((1,), (1,)), ((), ())), preferred_element_type=F32)


def _rms(x, g):
    return x * lax.rsqrt(jnp.mean(x * x, axis=-1, keepdims=True) + EPS) * g


def _norm_mod(x, g, sc, sh):
    return _rms(x, g) * (1.0 + sc) + sh


def _rope(x, cos_t, sin_t, half, group):
    lane = lax.broadcasted_iota(I32, x.shape, 1) % group
    partner = jnp.where(lane < half, pltpu.roll(x, LANES - half, 1), pltpu.roll(x, half, 1))
    return x * cos_t + partner * sin_t


VT_ONES = 16
VT_ROWS = LANES + VT_ONES
M_INIT = -1e20


def _store_vt(vt_ref, head, v, stage_ref):
    r0 = head * VT_ROWS
    stage_ref[...] = v
    vt_ref[0, r0:r0 + LANES, :] = stage_ref[...].T.astype(BF)
    vt_ref[0, r0 + LANES:r0 + VT_ROWS, :] = jnp.ones((VT_ONES, v.shape[0]), BF)


def _softmax_step(s_t, m, acc, vt):
    m_new = jnp.maximum(m, jnp.max(s_t, axis=0, keepdims=True))
    p = jnp.exp(s_t - m_new).astype(BF)
    acc = jnp.exp(m - m_new) * acc + _dot(vt, p)
    return m_new, acc


def _softmax_finish(acc):
    return (acc[:LANES] / acc[LANES:LANES + 1]).T


def _ada_kernel(c_ref, w_ref, b_ref, o_ref):
    c = c_ref[...]
    cond = c / (1.0 + jnp.exp(-c))
    o_ref[0] = _dot(cond.astype(BF), w_ref[0].astype(BF)) + b_ref[0]


def _ada(c, ada_w, ada_b):
    depth, d, n = ada_w.shape
    rows = 8
    c_pad = jnp.zeros((rows, d), F32).at[: c.shape[0]].set(c)
    tn = 1536
    out = pl.pallas_call(
        _ada_kernel,
        name="ada_mod",
        out_shape=jax.ShapeDtypeStruct((depth, rows, n), F32),
        grid=(depth, n // tn),
        in_specs=[
            pl.BlockSpec((rows, d), lambda l, j: (0, 0)),
            pl.BlockSpec((1, d, tn), lambda l, j: (l, 0, j)),
            pl.BlockSpec((1, 1, tn), lambda l, j: (l, 0, j)),
        ],
        out_specs=pl.BlockSpec((1, rows, tn), lambda l, j: (l, 0, j)),
        compiler_params=_params(("parallel", "parallel")),
    )(c_pad, ada_w, ada_b.reshape(depth, 1, n))
    return out[:, : c.shape[0]]


def _out_kernel(o_ref, w_ref, x_ref, g_ref, gate_ref, xo_ref):
    y = _dot(o_ref[0], w_ref[...])
    xo_ref[0] = x_ref[0] + gate_ref[0] * _rms(y, g_ref[...])


def _out_proj(o, w, x, g, gate, tm=512):
    b, s, d = x.shape
    k = o.shape[-1]
    row = lambda bi, i: (bi, i, 0)
    return pl.pallas_call(
        _out_kernel,
        name="out_proj",
        out_shape=jax.ShapeDtypeStruct(x.shape, F32),
        grid=(b, s // tm),
        in_specs=[
            pl.BlockSpec((1, tm, k), row),
            pl.BlockSpec((k, d), lambda bi, i: (0, 0)),
            pl.BlockSpec((1, tm, d), row),
            pl.BlockSpec((1, d), lambda bi, i: (0, 0)),
            pl.BlockSpec((1, 1, d), lambda bi, i: (bi, 0, 0)),
        ],
        out_specs=pl.BlockSpec((1, tm, d), row),
        compiler_params=_params(("parallel", "parallel")),
    )(o, w, x, g, gate)


def _sb_in_kernel(x_ref, g_ref, sc_ref, sh_ref, w_ref, o_ref, *, tn, n_q):
    h = _norm_mod(x_ref[0], g_ref[...], sc_ref[0], sh_ref[0]).astype(BF)
    scale = SB_HEAD_DIM ** -0.5
    for j in range(0, w_ref.shape[1], tn):
        y = _dot(h, w_ref[:, j:j + tn])
        if j < n_q:
            y = y * scale
        o_ref[0, :, j:j + tn] = y.astype(BF)


def _sb_in(x, g, sc, sh, w, tm=512):
    b, s, d = x.shape
    n = w.shape[1]
    row = lambda bi, i: (bi, i, 0)
    vec = lambda bi, i: (bi, 0, 0)
    return pl.pallas_call(
        functools.partial(_sb_in_kernel, tn=512, n_q=SB_HEADS * SB_HEAD_DIM),
        name="sb_in",
        out_shape=jax.ShapeDtypeStruct((b, s, n), BF),
        grid=(b, s // tm),
        in_specs=[
            pl.BlockSpec((1, tm, d), row),
            pl.BlockSpec((1, d), lambda bi, i: (0, 0)),
            pl.BlockSpec((1, 1, d), vec),
            pl.BlockSpec((1, 1, d), vec),
            pl.BlockSpec((d, n), lambda bi, i: (0, 0)),
        ],
        out_specs=pl.BlockSpec((1, tm, n), row),
        compiler_params=_params(("parallel", "parallel")),
    )(x, g, sc, sh, w)


def _sb_attn_kernel(q_ref, k_ref, v_ref, o_ref, *, t, heads):
    i = pl.program_id(2)
    d = SB_HEAD_DIM
    r_io = lax.broadcasted_iota(I32, (t, t + LANES), 0)
    c_io = lax.broadcasted_iota(I32, (t, t + LANES), 1)
    after = jnp.logical_or(r_io > c_io, c_io >= t).astype(BF)
    mask = lax.broadcasted_iota(I32, (t, t), 1) < lax.broadcasted_iota(I32, (t, t), 0)

    def block(j, hd, run, acc, diagonal):
        start = pl.multiple_of(j * t, t)
        cols = slice(hd * d, (hd + 1) * d)
        z = _dot_t(q_ref[0, :, cols], k_ref[0, pl.ds(start, t), cols])
        sp = jnp.log1p(jnp.exp(-jnp.abs(z)))
        log_beta = jnp.minimum(z, 0.0) - sp
        log_fail = -jnp.maximum(z, 0.0) - sp
        if diagonal:
            log_fail = jnp.where(mask, log_fail, 0.0)
        hi = log_fail.astype(BF)
        lo = (log_fail - hi.astype(F32)).astype(BF)
        sums = _dot(hi, after) + _dot(lo, after)
        tail = jnp.concatenate(
            [sums[:, u * LANES:(u + 1) * LANES] + run for u in range(t // LANES)], axis=1)
        a = jnp.exp(log_beta + tail)
        if diagonal:
            a = jnp.where(mask, a, 0.0)
        acc = acc + _dot(a.astype(BF), v_ref[0, pl.ds(start, t), cols])
        return run + sums[:, t:], acc

    def blocks(j, state, diagonal):
        out = []
        for hd in range(heads):
            out.extend(block(j, hd, state[2 * hd], state[2 * hd + 1], diagonal))
        return tuple(out)

    def live(state):
        top = state[0]
        for hd in range(1, heads):
            top = jnp.maximum(top, state[2 * hd])
        return jnp.max(top) > SB_UNDERFLOW

    state = blocks(i, (jnp.zeros((t, LANES), F32), jnp.zeros((t, d), F32)) * heads, True)

    def cond(carry):
        return jnp.logical_and(carry[0] >= 0, carry[1])

    def body(carry):
        state = blocks(carry[0], carry[2:], False)
        return (carry[0] - 1, live(state)) + state

    res = lax.while_loop(cond, body, (i - 1, live(state)) + state)
    for hd in range(heads):
        o_ref[0, :, hd * d:(hd + 1) * d] = res[2 + 2 * hd + 1].astype(BF)


def _sb_attn(qkv, t=256, heads=2):
    b, s, _ = qkv.shape
    ng = SB_HEADS // heads
    w = heads * SB_HEAD_DIM
    return pl.pallas_call(
        functools.partial(_sb_attn_kernel, t=t, heads=heads),
        name="sb_attn",
        out_shape=jax.ShapeDtypeStruct((b, s, SB_HEADS * SB_HEAD_DIM), BF),
        grid=(b, ng, s // t),
        in_specs=[
            pl.BlockSpec((1, t, w), lambda bi, h, i: (bi, i, h)),
            pl.BlockSpec((1, s, w), lambda bi, h, i: (bi, 0, ng + h)),
            pl.BlockSpec((1, s, w), lambda bi, h, i: (bi, 0, 2 * ng + h)),
        ],
        out_specs=pl.BlockSpec((1, t, w), lambda bi, h, i: (bi, i, h)),
        compiler_params=_params(("parallel", "parallel", "arbitrary")),
    )(qkv, qkv, qkv)


def _mla_in_kernel(x_ref, g_ref, sc_ref, sh_ref, w_in_ref, gq_ref, wq_ref, gkv_ref, wkv_ref,
                   cos_ref, sin_ref, q_ref, k_ref, v_ref, stage_ref):
    h = _norm_mod(x_ref[0], g_ref[...], sc_ref[0], sh_ref[0]).astype(BF)
    lat = _dot(h, w_in_ref[...])
    cos_t, sin_t = cos_ref[0], sin_ref[0]
    half = MLA_ROPE_DIM // 2
    c_q = _rms(lat[:, :MLA_Q_LORA], gq_ref[...]).astype(BF)
    c_kv = _rms(lat[:, MLA_Q_LORA:MLA_Q_LORA + MLA_KV_LORA], gkv_ref[...]).astype(BF)
    k_rope = _rope(lat[:, MLA_Q_LORA + MLA_KV_LORA:], cos_t, sin_t, half, LANES).astype(BF)
    scale = (MLA_NOPE_DIM + MLA_ROPE_DIM) ** -0.5
    for hd in range(MLA_HEADS):
        c0 = hd * MLA_QK_PAD
        qh = _dot(c_q, wq_ref[:, c0:c0 + MLA_QK_PAD])
        q_ref[0, :, c0:c0 + LANES] = (qh[:, :LANES] * scale).astype(BF)
        q_ref[0, :, c0 + LANES:c0 + MLA_QK_PAD] = (
            _rope(qh[:, LANES:], cos_t, sin_t, half, LANES) * scale).astype(BF)
        k_ref[0, :, c0:c0 + LANES] = _dot(c_kv, wkv_ref[:, hd * LANES:(hd + 1) * LANES]).astype(BF)
        k_ref[0, :, c0 + LANES:c0 + MLA_QK_PAD] = k_rope
    nk = MLA_HEADS * MLA_NOPE_DIM
    for hd in range(MLA_HEADS):
        v = _dot(c_kv, wkv_ref[:, nk + hd * LANES:nk + (hd + 1) * LANES])
        _store_vt(v_ref, hd, v, stage_ref)


def _mla_in(x, g, sc, sh, w_in, g_q, w_q, g_kv, w_kv, cos_t, sin_t, tm=512):
    b, s, d = x.shape
    row = lambda bi, i: (bi, i, 0)
    vec = lambda bi, i: (bi, 0, 0)
    full = lambda bi, i: (0, 0)
    nqk = MLA_HEADS * MLA_QK_PAD
    nv = MLA_HEADS * VT_ROWS
    return pl.pallas_call(
        _mla_in_kernel,
        name="mla_in",
        out_shape=(jax.ShapeDtypeStruct((b, s, nqk), BF), jax.ShapeDtypeStruct((b, s, nqk), BF),
                   jax.ShapeDtypeStruct((b, nv, s), BF)),
        grid=(b, s // tm),
        in_specs=[
            pl.BlockSpec((1, tm, d), row),
            pl.BlockSpec((1, d), full),
            pl.BlockSpec((1, 1, d), vec),
            pl.BlockSpec((1, 1, d), vec),
            pl.BlockSpec(w_in.shape, full),
            pl.BlockSpec(g_q.shape, full),
            pl.BlockSpec(w_q.shape, full),
            pl.BlockSpec(g_kv.shape, full),
            pl.BlockSpec(w_kv.shape, full),
            pl.BlockSpec((1, tm, LANES), row),
            pl.BlockSpec((1, tm, LANES), row),
        ],
        out_specs=(pl.BlockSpec((1, tm, nqk), row), pl.BlockSpec((1, tm, nqk), row),
                   pl.BlockSpec((1, nv, tm), lambda bi, i: (bi, 0, i))),
        scratch_shapes=[pltpu.VMEM((tm, LANES), F32)],
        compiler_params=_params(("parallel", "parallel")),
    )(x, g, sc, sh, w_in, g_q, w_q, g_kv, w_kv, cos_t, sin_t)


def _flash_kernel(q_ref, k_ref, vt_ref, o_ref, *, t, per_trip):
    i = pl.program_id(2)
    q = q_ref[0]

    def scores(j):
        start = pl.multiple_of(j * t, t)
        return _dot_t(k_ref[0, pl.ds(start, t), :], q)

    def step(j, s_t, m, acc):
        start = pl.multiple_of(j * t, t)
        return _softmax_step(s_t, m, acc, vt_ref[0, :, pl.ds(start, t)])

    def group(jj, carry):
        s_all = [scores(per_trip * jj + u) for u in range(per_trip)]
        for u in range(per_trip):
            carry = step(per_trip * jj + u, s_all[u], *carry)
        return carry

    init = (jnp.full((1, t), M_INIT, F32), jnp.zeros((VT_ROWS, t), F32))
    carry = lax.fori_loop(0, i // per_trip, group, init)
    carry = lax.fori_loop(i // per_trip * per_trip, i, lambda j, c: step(j, scores(j), *c), carry)
    key_io = lax.broadcasted_iota(I32, (t, t), 0)
    qry_io = lax.broadcasted_iota(I32, (t, t), 1)
    m, acc = step(i, jnp.where(key_io <= qry_io, scores(i), NEG_BIG), *carry)
    o_ref[0] = _softmax_finish(acc).astype(BF)


def _flash(q, k, vt, heads, dk, t=512):
    b, s, _ = q.shape
    return pl.pallas_call(
        functools.partial(_flash_kernel, t=t, per_trip=4),
        name="flash_attn",
        out_shape=jax.ShapeDtypeStruct((b, s, heads * LANES), BF),
        grid=(b, heads, s // t),
        in_specs=[
            pl.BlockSpec((1, t, dk), lambda bi, h, i: (bi, i, h)),
            pl.BlockSpec((1, s, dk), lambda bi, h, i: (bi, 0, h)),
            pl.BlockSpec((1, VT_ROWS, s), lambda bi, h, i: (bi, h, 0)),
        ],
        out_specs=pl.BlockSpec((1, t, LANES), lambda bi, h, i: (bi, i, h)),
        compiler_params=_params(("parallel", "parallel", "arbitrary")),
    )(q, k, vt)


INT_MAX = 2 ** 31 - 1
VALUE_STEPS = 24


def _float_to_key(x):
    bits = pltpu.bitcast(x, I32)
    return jnp.where(bits < 0, bits ^ 0x7FFFFFFF, bits)


def _key_to_float(key):
    return pltpu.bitcast(jnp.where(key < 0, key ^ 0x7FFFFFFF, key), F32)


DSA_NQ = DSA_HEADS * DSA_HEAD_DIM
DSA_NKV = DSA_KV_HEADS * DSA_HEAD_DIM
DSA_NQI = IDX_HEADS * IDX_DIM
DSA_COLS = DSA_NQ + 2 * DSA_NKV + DSA_NQI + 2 * LANES


def _dsa_in_kernel(x_ref, g_ref, sc_ref, sh_ref, w_ref, gk_ref, cos_a_ref, sin_a_ref, cos_i_ref,
                   sin_i_ref, q_ref, k_ref, v_ref, qi_ref, ki_ref, wi_ref, stage_ref):
    h = _norm_mod(x_ref[0], g_ref[...], sc_ref[0], sh_ref[0]).astype(BF)
    cos_a, sin_a = cos_a_ref[0], sin_a_ref[0]
    cos_i, sin_i = cos_i_ref[0], sin_i_ref[0]
    scale = DSA_HEAD_DIM ** -0.5
    ha, hi = DSA_ROT_DIM // 2, IDX_ROT_DIM // 2
    for hd in range(DSA_HEADS):
        c0 = hd * LANES
        y = _dot(h, w_ref[:, c0:c0 + LANES])
        q_ref[0, :, c0:c0 + LANES] = (_rope(y, cos_a, sin_a, ha, LANES) * scale).astype(BF)
    for hd in range(DSA_KV_HEADS):
        c0 = hd * LANES
        y = _dot(h, w_ref[:, DSA_NQ + c0:DSA_NQ + c0 + LANES])
        k_ref[0, :, c0:c0 + LANES] = _rope(y, cos_a, sin_a, ha, LANES).astype(BF)
    off = DSA_NQ + DSA_NKV
    for hd in range(DSA_KV_HEADS):
        _store_vt(v_ref, hd, _dot(h, w_ref[:, off + hd * LANES:off + (hd + 1) * LANES]), stage_ref)
    off += DSA_NKV
    for p in range(DSA_NQI // LANES):
        c0 = p * LANES
        y = _dot(h, w_ref[:, off + c0:off + c0 + LANES])
        qi_ref[0, :, c0:c0 + LANES] = _rope(y, cos_i, sin_i, hi, IDX_DIM).astype(BF)
    off += DSA_NQI
    y = _dot(h, w_ref[:, off:off + LANES])
    y = y * lax.rsqrt(jnp.sum(y * y, axis=-1, keepdims=True) * (1.0 / IDX_DIM) + EPS) * gk_ref[...]
    y = _rope(y, cos_i, sin_i, hi, IDX_DIM)
    ki_ref[0] = (y + pltpu.roll(y, IDX_DIM, 1)).astype(BF)
    off += LANES
    wi_ref[0] = _dot(h, w_ref[:, off:off + LANES]) * (IDX_HEADS ** -0.5 * IDX_DIM ** -0.5)


def _dsa_in(x, g, sc, sh, w, gk, cos_a, sin_a, cos_i, sin_i, tm=512):
    b, s, d = x.shape
    row = lambda bi, i: (bi, i, 0)
    vec = lambda bi, i: (bi, 0, 0)
    full = lambda bi, i: (0, 0)
    tab = pl.BlockSpec((1, tm, LANES), row)
    nvt = DSA_KV_HEADS * VT_ROWS

    def spec(n):
        return pl.BlockSpec((1, tm, n), row)

    def sds(n, dt=BF):
        return jax.ShapeDtypeStruct((b, s, n), dt)

    return pl.pallas_call(
        _dsa_in_kernel,
        name="dsa_in",
        out_shape=(sds(DSA_NQ), sds(DSA_NKV), jax.ShapeDtypeStruct((b, nvt, s), BF), sds(DSA_NQI),
                   sds(LANES), sds(LANES, F32)),
        grid=(b, s // tm),
        in_specs=[
            pl.BlockSpec((1, tm, d), row),
            pl.BlockSpec((1, d), full),
            pl.BlockSpec((1, 1, d), vec),
            pl.BlockSpec((1, 1, d), vec),
            pl.BlockSpec(w.shape, full),
            pl.BlockSpec(gk.shape, full),
            tab, tab, tab, tab,
        ],
        out_specs=(spec(DSA_NQ), spec(DSA_NKV), pl.BlockSpec((1, nvt, tm), lambda bi, i: (bi, 0, i)),
                   spec(DSA_NQI), spec(LANES), spec(LANES)),
        scratch_shapes=[pltpu.VMEM((tm, LANES), F32)],
        compiler_params=_params(("parallel", "parallel")),
    )(x, g, sc, sh, w, gk, cos_a, sin_a, cos_i, sin_i)


def _dsa_attn_kernel(q_ref, qi_ref, wi_ref, ki_ref, k_ref, vt_ref, o_ref, key_scr, *, tq, tk, n_sel):
    i = pl.program_id(1)
    n_chunks = (i * tq + tq + tk - 1) // tk
    group = DSA_HEADS // DSA_KV_HEADS
    key_io = lax.broadcasted_iota(I32, (tk, tq), 0)
    qry = lax.broadcasted_iota(I32, (1, tq), 1) + i * tq
    lane = lax.broadcasted_iota(I32, (tq, LANES), 1)

    qi = qi_ref[0]
    zero = jnp.zeros_like(qi[:, :LANES])
    stacked = []
    for hd in range(IDX_HEADS):
        pair = qi[:, (hd // 2) * LANES:(hd // 2 + 1) * LANES]
        low = (hd % 2) == 0
        stacked.append(jnp.where((lane < IDX_DIM) == low, pair, zero))
    q_all = jnp.concatenate(stacked, axis=0)
    w_t = wi_ref[0].T

    def fold8(x, op):
        for shift in (4, 2, 1):
            x = op(x, pltpu.roll(x, shift, 0))
        return x

    def score_chunk(c, carry):
        k_min, k_max = carry
        start = pl.multiple_of(c * tk, tk)
        logits = _dot_t(ki_ref[0, pl.ds(start, tk), :], q_all)
        score = jnp.zeros((tk, tq), F32)
        for hd in range(IDX_HEADS):
            score = score + jnp.maximum(logits[:, hd * tq:(hd + 1) * tq], 0.0) * w_t[hd:hd + 1]
        key = _float_to_key(score)
        key = jnp.where(score == 0.0, 0, key)
        causal = key_io + start <= qry
        key_scr[pl.ds(start, tk), :] = jnp.where(causal, key, INT_MIN)
        k_min = jnp.minimum(k_min, jnp.min(
            jnp.where(causal, key, INT_MAX).reshape(tk // 8, 8, tq), axis=0))
        k_max = jnp.maximum(k_max, jnp.max(
            jnp.where(causal, key, INT_MIN).reshape(tk // 8, 8, tq), axis=0))
        return k_min, k_max

    k_min, k_max = lax.fori_loop(
        0, n_chunks, score_chunk,
        (jnp.full((8, tq), INT_MAX, I32), jnp.full((8, tq), INT_MIN, I32)))
    k_min, k_max = fold8(k_min, jnp.minimum), fold8(k_max, jnp.maximum)

    tc = min(tk, 512)

    def count(pred):
        def chunk(c, cnt):
            start = pl.multiple_of(c * tc, tc)
            key = key_scr[pl.ds(start, tc), :].reshape(tc // 8, 8, tq)
            krow = (lax.broadcasted_iota(I32, (tc, tq), 0) + start).reshape(tc // 8, 8, tq)
            return cnt + jnp.sum(pred(key, krow).astype(I32), axis=0)
        cnt = lax.fori_loop(0, n_chunks * (tk // tc), chunk, jnp.zeros((8, tq), I32))
        return fold8(cnt, jnp.add)

    def search_step(it, lo, hi, n_lo, n_hi, done):
        f_lo, f_hi = _key_to_float(lo), _key_to_float(hi)
        frac = ((n_lo - n_sel).astype(F32) - 0.5) / (n_lo - n_hi).astype(F32)
        by_count = _float_to_key(f_lo + frac * (f_hi - f_lo))
        by_value = _float_to_key(0.5 * f_lo + 0.5 * f_hi)
        by_key = (lo >> 1) + (hi >> 1) + (lo & hi & 1)
        cand = jnp.where(it >= VALUE_STEPS, by_key, jnp.where(it % 2 == 0, by_count, by_value))
        cand = jnp.where(it < 2, it, cand)
        cand = jnp.minimum(jnp.maximum(cand, lo + 1), hi - 1)
        n_c = count(lambda key, krow: key >= cand)
        live = done == 0
        up = jnp.logical_and(live, n_c >= n_sel)
        down = jnp.logical_and(live, n_c < n_sel)
        lo, n_lo = jnp.where(up, cand, lo), jnp.where(up, n_c, n_lo)
        hi, n_hi = jnp.where(down, cand, hi), jnp.where(down, n_c, n_hi)
        done = jnp.where(jnp.logical_or(n_lo == n_sel, hi == lo + 1), 1, done)
        return lo, hi, n_lo, n_hi, done

    def search_cond(carry):
        return jnp.logical_and(carry[0] < VALUE_STEPS + 34, jnp.min(carry[-1]) == 0)

    def search_body(carry):
        it, state = carry[0], carry[1:]
        for u in range(4):
            state = search_step(it + u, *state)
        return (it + 4,) + state

    n_lo0 = jnp.zeros((8, tq), I32) + (qry + 1)
    hi0 = k_max + 1
    done0 = jnp.where(jnp.logical_or(n_lo0 <= n_sel, hi0 == k_min + 1), 1, 0)
    res = lax.while_loop(search_cond, search_body,
                         (jnp.int32(0), k_min, hi0, n_lo0, jnp.zeros((8, tq), I32), done0))
    thr, n_at = res[1], res[3]
    tie_bits = key_scr.shape[0].bit_length()

    def tie_search(_):
        want = n_sel - count(lambda key, krow: key > thr)

        def cut_bit(b, cut):
            cand = cut + lax.shift_left(jnp.int32(1), tie_bits - 1 - b)
            n_tie = count(lambda key, krow: jnp.logical_and(key == thr, krow < cand))
            return jnp.where(n_tie <= want, cand, cut)
        return lax.fori_loop(0, tie_bits, cut_bit, jnp.zeros((8, tq), I32))

    crowded = jnp.max(jnp.where(jnp.logical_and(n_at > n_sel, thr > INT_MIN), 1, 0)) > 0
    cut = lax.cond(crowded, tie_search, lambda _: jnp.full((8, tq), 2 ** tie_bits - 1, I32), 0)
    thr, cut = thr[:1], cut[:1]

    q = q_ref[0]
    q_g = [jnp.concatenate([q[:, (g * group + n) * LANES:(g * group + n + 1) * LANES]
                            for n in range(group)], axis=0) for g in range(DSA_KV_HEADS)]

    def scores(c, g):
        start = pl.multiple_of(c * tk, tk)
        return _dot_t(k_ref[0, pl.ds(start, tk), g * LANES:(g + 1) * LANES], q_g[g])

    def attn_chunk(c, carry):
        start = pl.multiple_of(c * tk, tk)
        s_t = [scores(c, g) for g in range(DSA_KV_HEADS)]
        key = key_scr[pl.ds(start, tk), :]
        krow = key_io + start
        sel = jnp.logical_or(key > thr, jnp.logical_and(key == thr, krow < cut))
        sel = jnp.logical_and(sel, krow <= qry)
        bias = jnp.where(sel, 0.0, NEG_BIG)
        bias = jnp.concatenate([bias] * group, axis=1)
        out = []
        for g in range(DSA_KV_HEADS):
            m, acc = carry[2 * g:2 * g + 2]
            out.extend(_softmax_step(
                s_t[g] + bias, m, acc, vt_ref[0, g * VT_ROWS:(g + 1) * VT_ROWS, pl.ds(start, tk)]))
        return tuple(out)

    init = (jnp.full((1, group * tq), M_INIT, F32), jnp.zeros((VT_ROWS, group * tq), F32))
    res = lax.fori_loop(0, n_chunks, attn_chunk, init * DSA_KV_HEADS)
    for g in range(DSA_KV_HEADS):
        acc = res[2 * g + 1]
        for n in range(group):
            hd = g * group + n
            o_ref[0, :, hd * LANES:(hd + 1) * LANES] = _softmax_finish(
                acc[:, n * tq:(n + 1) * tq]).astype(BF)


def _dsa_attn(q, qi, wi, ki, k, vt, tq=128, tk=1024):
    b, s, _ = q.shape
    n_sel = min(DSA_TOPK_MAX, s // 4)
    tk = min(tk, s)
    row = lambda bi, i: (bi, i, 0)
    whole = lambda bi, i: (bi, 0, 0)
    return pl.pallas_call(
        functools.partial(_dsa_attn_kernel, tq=tq, tk=tk, n_sel=n_sel),
        name="dsa_attn",
        out_shape=jax.ShapeDtypeStruct((b, s, DSA_NQ), BF),
        grid=(b, s // tq),
        in_specs=[
            pl.BlockSpec((1, tq, DSA_NQ), row),
            pl.BlockSpec((1, tq, DSA_NQI), row),
            pl.BlockSpec((1, tq, LANES), row),
            pl.BlockSpec((1, s, LANES), whole),
            pl.BlockSpec((1, s, DSA_NKV), whole),
            pl.BlockSpec((1, DSA_KV_HEADS * VT_ROWS, s), whole),
        ],
        out_specs=pl.BlockSpec((1, tq, DSA_NQ), row),
        scratch_shapes=[pltpu.VMEM((s, tq), I32)],
        compiler_params=_params(("parallel", "arbitrary")),
    )(q, qi, wi, ki, k, vt)


def _ffn_kernel(x_ref, g_ref, sc_ref, sh_ref, wg_ref, wu_ref, wd_ref, gp_ref, gate_ref, xo_ref,
                h_scr, acc_scr):
    e = pl.program_id(2)

    @pl.when(e == 0)
    def _():
        h_scr[...] = _norm_mod(x_ref[0], g_ref[...], sc_ref[0], sh_ref[0]).astype(BF)
        acc_scr[...] = jnp.zeros(acc_scr.shape, F32)

    h = h_scr[...]
    a = _dot(h, wg_ref[0])
    u = _dot(h, wu_ref[0])
    act = (a / (1.0 + jnp.exp(-a)) * u).astype(BF)
    acc_scr[...] += _dot(act, wd_ref[0])

    @pl.when(e == pl.num_programs(2) - 1)
    def _():
        xo_ref[0] = x_ref[0] + gate_ref[0] * _rms(acc_scr[...], gp_ref[...])


def _ffn(x, g, sc, sh, w_gate, w_up, w_down, g_post, gate, tm=512):
    b, s, d = x.shape
    n_e, _, f = w_gate.shape
    row = lambda bi, i, e: (bi, i, 0)
    vec = lambda bi, i, e: (bi, 0, 0)
    full = lambda bi, i, e: (0, 0)
    slab = lambda bi, i, e: (e, 0, 0)
    return pl.pallas_call(
        _ffn_kernel,
        name="ffn",
        out_shape=jax.ShapeDtypeStruct(x.shape, F32),
        grid=(b, s // tm, n_e),
        in_specs=[
            pl.BlockSpec((1, tm, d), row),
            pl.BlockSpec((1, d), full),
            pl.BlockSpec((1, 1, d), vec),
            pl.BlockSpec((1, 1, d), vec),
            pl.BlockSpec((1, d, f), slab),
            pl.BlockSpec((1, d, f), slab),
            pl.BlockSpec((1, f, d), slab),
            pl.BlockSpec((1, d), full),
            pl.BlockSpec((1, 1, d), vec),
        ],
        out_specs=pl.BlockSpec((1, tm, d), row),
        scratch_shapes=[pltpu.VMEM((tm, d), BF), pltpu.VMEM((tm, d), F32)],
        compiler_params=_params(("parallel", "parallel", "arbitrary")),
    )(x, g, sc, sh, w_gate, w_up, w_down, g_post, gate)


MOE_TILE = 1024
MOE_ROWS = 256


def _moe_kernel(x_ref, g_ref, sc_ref, sh_ref, wr_ref, br_ref, wg_ref, wu_ref, wd_ref, gp_ref,
                gate_ref, xo_ref, h_scr, gate_c, rank_c, rank_r, acc_scr, *, rows):
    e = pl.program_id(2)
    t = h_scr.shape[0]
    lane = lax.broadcasted_iota(I32, (t, LANES), 1)

    @pl.when(e == 0)
    def _():
        h32 = _norm_mod(x_ref[0], g_ref[...], sc_ref[0], sh_ref[0])
        h = h32.astype(BF)
        h_scr[...] = h
        acc_scr[...] = jnp.zeros(acc_scr.shape, F32)
        h_lo = (h32 - h.astype(F32)).astype(BF)
        logits = _dot(h, wr_ref[0]) + (_dot(h, wr_ref[1]) + _dot(h_lo, wr_ref[0]))
        logits = jnp.where(lane < N_EXPERTS, logits + br_ref[...], -jnp.inf)
        m1 = jnp.max(logits, axis=-1, keepdims=True)
        i1 = jnp.min(jnp.where(logits == m1, lane, LANES), axis=-1, keepdims=True)
        rest = jnp.where(lane == i1, -jnp.inf, logits)
        m2 = jnp.max(rest, axis=-1, keepdims=True)
        i2 = jnp.min(jnp.where(rest == m2, lane, LANES), axis=-1, keepdims=True)
        e2 = jnp.exp(m2 - m1)
        w1 = 1.0 / (1.0 + e2)
        gate_c[...] = jnp.where(lane == i1, w1, 0.0) + jnp.where(lane == i2, e2 * w1, 0.0)
        sel = jnp.logical_or(lane == i1, lane == i2)
        earlier = (lax.broadcasted_iota(I32, (t, t), 1) < lax.broadcasted_iota(I32, (t, t), 0))
        rank = _dot(earlier.astype(BF), jnp.where(sel, 1.0, 0.0).astype(BF))
        rank = jnp.where(sel, rank, -1.0)
        rank_c[...] = rank
        rank_r[...] = rank.T

    rank_row = rank_r[pl.ds(e, 1), :]
    on_e = lane == e
    rank_col = jnp.sum(jnp.where(on_e, rank_c[...], 0.0), axis=-1, keepdims=True)
    gate_col = jnp.sum(jnp.where(on_e, gate_c[...], 0.0), axis=-1, keepdims=True)
    n_tokens = jnp.max(rank_row).astype(I32) + 1

    def expert_pass(first_row, rows):
        base = first_row.astype(F32)
        row_io = lax.broadcasted_iota(I32, (rows, t), 0).astype(F32) + base
        pick = jnp.where(rank_row == row_io, 1.0, 0.0).astype(BF)
        hs = _dot(pick, h_scr[...]).astype(BF)
        a = _dot(hs, wg_ref[0])
        u = _dot(hs, wu_ref[0])
        act = (a / (1.0 + jnp.exp(-a)) * u).astype(BF)
        y = _dot(act, wd_ref[0]).astype(BF)
        col_io = lax.broadcasted_iota(I32, (t, rows), 1).astype(F32) + base
        place = jnp.where(rank_col == col_io, 1.0, 0.0).astype(BF)
        acc_scr[...] += gate_col * _dot(place, y)

    half = rows // 2
    n_half = (n_tokens + half - 1) // half

    def full_pass(p, _):
        expert_pass(p * rows, rows)
        return 0

    lax.fori_loop(0, n_half // 2, full_pass, 0)

    @pl.when(n_half % 2 == 1)
    def _():
        expert_pass((n_half // 2) * rows, half)

    @pl.when(e == pl.num_programs(2) - 1)
    def _():
        xo_ref[0] = x_ref[0] + gate_ref[0] * _rms(acc_scr[...], gp_ref[...])


def _moe(x, g, sc, sh, w_router, b_router, w_gate, w_up, w_down, g_post, gate):
    b, s, d = x.shape
    n_e, _, f = w_gate.shape
    tm = min(MOE_TILE, s)
    row = lambda bi, i, e: (bi, i, 0)
    vec = lambda bi, i, e: (bi, 0, 0)
    full = lambda bi, i, e: (0, 0)
    slab = lambda bi, i, e: (e, 0, 0)
    once = pl.Buffered(1)
    return pl.pallas_call(
        functools.partial(_moe_kernel, rows=min(MOE_ROWS, tm)),
        name="moe",
        out_shape=jax.ShapeDtypeStruct(x.shape, F32),
        grid=(b, s // tm, n_e),
        in_specs=[
            pl.BlockSpec((1, tm, d), row, pipeline_mode=once),
            pl.BlockSpec((1, d), full),
            pl.BlockSpec((1, 1, d), vec),
            pl.BlockSpec((1, 1, d), vec),
            pl.BlockSpec((2, d, LANES), lambda bi, i, e: (0, 0, 0)),
            pl.BlockSpec((1, LANES), full),
            pl.BlockSpec((1, d, f), slab),
            pl.BlockSpec((1, d, f), slab),
            pl.BlockSpec((1, f, d), slab),
            pl.BlockSpec((1, d), full),
            pl.BlockSpec((1, 1, d), vec),
        ],
        out_specs=pl.BlockSpec((1, tm, d), row, pipeline_mode=once),
        scratch_shapes=[
            pltpu.VMEM((tm, d), BF),
            pltpu.VMEM((tm, LANES), F32),
            pltpu.VMEM((tm, LANES), F32),
            pltpu.VMEM((LANES, tm), F32),
            pltpu.VMEM((tm, d), F32),
        ],
        compiler_params=_params(("parallel", "parallel", "arbitrary")),
    )(x, g, sc, sh, w_router, b_router, w_gate, w_up, w_down, g_post, gate)


def _rope_tables(positions, rot_dim, group):
    half = rot_dim // 2
    inv_freq = ROPE_THETA ** (-jnp.arange(0, rot_dim, 2, dtype=F32) / rot_dim)
    ang = positions.astype(F32)[..., None] * inv_freq
    cos, sin = jnp.cos(ang), jnp.sin(ang)
    pad = group - rot_dim
    ones = jnp.ones(cos.shape[:-1] + (pad,), F32)
    zeros = jnp.zeros(cos.shape[:-1] + (pad,), F32)
    cos_t = jnp.concatenate([cos, cos, ones], axis=-1)
    sin_t = jnp.concatenate([-sin, sin, zeros], axis=-1)
    reps = LANES // group
    return jnp.tile(cos_t, (1, 1, reps)), jnp.tile(sin_t, (1, 1, reps))


def _pad_cols(w, n):
    return jnp.pad(w, ((0, 0), (0, n - w.shape[1])))


def _split_gu(w_gu, f):
    return w_gu[..., :f].astype(BF), w_gu[..., f:].astype(BF)


def kernel(x, c, positions, ada_w, ada_b, norm_g, sb_w_in, sb_w_out, mla_w_in, mla_g_q, mla_w_q_up,
           mla_g_kv, mla_w_kv_up, mla_w_out, dsa_w_in, dsa_g_kidx, dsa_w_out, ffn_w_gu, ffn_w_down,
           moe_w_router, moe_b_router, moe_w_gu, moe_w_down):
    depth = ada_w.shape[0]
    b, s, d = x.shape
    mod = _ada(c, ada_w, ada_b)
    cos_m, sin_m = _rope_tables(positions, MLA_ROPE_DIM, LANES)
    cos_a, sin_a = _rope_tables(positions, DSA_ROT_DIM, LANES)
    cos_i, sin_i = _rope_tables(positions, IDX_ROT_DIM, IDX_DIM)
    counters = [0, 0, 0]
    for layer in range(depth):
        sh_m, sc_m, g_m, sh_f, sc_f, g_f = [
            mod[layer, :, n * d:(n + 1) * d].reshape(b, 1, d) for n in range(6)]
        gn = norm_g[layer].reshape(4, 1, d)
        kind = layer % N_MIXERS
        j = counters[kind]
        counters[kind] += 1
        if kind == 0:
            qkv = _sb_in(x, gn[0], sc_m, sh_m, sb_w_in[j].astype(BF))
            o = _sb_attn(qkv)
            w_out = sb_w_out[j]
        elif kind == 1:
            w_in = _pad_cols(mla_w_in[j], 4 * LANES).astype(BF)
            w_q = mla_w_q_up[j].reshape(MLA_Q_LORA, MLA_HEADS, MLA_NOPE_DIM + MLA_ROPE_DIM)
            w_q = jnp.pad(w_q, ((0, 0), (0, 0), (0, MLA_QK_PAD - w_q.shape[-1])))
            w_q = w_q.reshape(MLA_Q_LORA, MLA_HEADS * MLA_QK_PAD).astype(BF)
            w_kv = mla_w_kv_up[j].reshape(MLA_KV_LORA, MLA_HEADS, MLA_NOPE_DIM + MLA_V_DIM)
            w_kv = jnp.concatenate([w_kv[..., :MLA_NOPE_DIM].reshape(MLA_KV_LORA, -1),
                                    w_kv[..., MLA_NOPE_DIM:].reshape(MLA_KV_LORA, -1)], axis=1)
            q, k, v = _mla_in(x, gn[0], sc_m, sh_m, w_in, mla_g_q[j].reshape(1, -1), w_q,
                              mla_g_kv[j].reshape(1, -1), w_kv.astype(BF), cos_m, sin_m)
            o = _flash(q, k, v, MLA_HEADS, MLA_QK_PAD)
            w_out = mla_w_out[j]
        else:
            w = dsa_w_in[j]
            n_main = DSA_NQ + 2 * DSA_NKV + DSA_NQI
            w = jnp.concatenate([w[:, :n_main], _pad_cols(w[:, n_main:n_main + IDX_DIM], LANES),
                                 _pad_cols(w[:, n_main + IDX_DIM:], LANES)], axis=1).astype(BF)
            gk = _pad_cols(dsa_g_kidx[j].reshape(1, -1), LANES)
            q, k, v, qi, ki, wi = _dsa_in(x, gn[0], sc_m, sh_m, w, gk, cos_a, sin_a, cos_i, sin_i)
            o = _dsa_attn(q, qi, wi, ki, k, v)
            w_out = dsa_w_out[j]
        x = _out_proj(o, w_out.astype(BF), x, gn[1], g_m)
        f = layer // 2
        if layer % 2 == 0:
            half = D_FF // 2
            w_gate, w_up = _split_gu(ffn_w_gu[f], D_FF)
            w_gate = w_gate.reshape(d, 2, half).transpose(1, 0, 2)
            w_up = w_up.reshape(d, 2, half).transpose(1, 0, 2)
            w_down = ffn_w_down[f].astype(BF).reshape(2, half, d)
            x = _ffn(x, gn[2], sc_f, sh_f, w_gate, w_up, w_down, gn[3], g_f)
        else:
            w_gate, w_up = _split_gu(moe_w_gu[f], D_FF_EXPERT)
            w_r = _pad_cols(moe_w_router[f], LANES)
            w_r_hi = w_r.astype(BF)
            w_r = jnp.stack([w_r_hi, (w_r - w_r_hi.astype(F32)).astype(BF)])
            b_r = _pad_cols(moe_b_router[f].reshape(1, -1), LANES)
            x = _moe(x, gn[2], sc_f, sh_f, w_r, b_r, w_gate, w_up, moe_w_down[f].astype(BF), gn[3],
                     g_f)
    return x
```

```python
import functools

import jax
import jax.numpy as jnp
from jax import lax
from jax.experimental import pallas as pl
from jax.experimental.pallas import tpu as pltpu

BF = jnp.bfloat16
F32 = jnp.float32
I32 = jnp.int32

D_MODEL = 1024
N_MIXERS = 3
ROPE_THETA = 500000.0
EPS = 1e-6
LANES = 128

SB_HEADS = 8
SB_HEAD_DIM = 128
MLA_HEADS = 8
MLA_Q_LORA = 256
MLA_KV_LORA = 128
MLA_NOPE_DIM = 128
MLA_ROPE_DIM = 64
MLA_V_DIM = 128
MLA_QK_PAD = 256
DSA_HEADS = 8
DSA_KV_HEADS = 2
DSA_HEAD_DIM = 128
DSA_ROT_DIM = 32
IDX_HEADS = 8
IDX_DIM = 64
IDX_ROT_DIM = 16
DSA_TOPK_MAX = 256
D_FF = 2816
N_EXPERTS = 8
D_FF_EXPERT = 1408

VMEM_LIMIT = 56 * 1024 * 1024
INT_MIN = -(2 ** 31)
SB_UNDERFLOW = -104.0
NEG_BIG = -1e30


def _params(sem):
    return pltpu.CompilerParams(dimension_semantics=sem, vmem_limit_bytes=VMEM_LIMIT)


def _dot(a, b):
    return jnp.dot(a, b, preferred_element_type=F32)


def _dot_t(a, b):
    return lax.dot_general(a, b, (((1,), (1,)), ((), ())), preferred_element_type=F32)


def _rms(x, g):
    return x * lax.rsqrt(jnp.mean(x * x, axis=-1, keepdims=True) + EPS) * g


def _norm_mod(x, g, sc, sh):
    return _rms(x, g) * (1.0 + sc) + sh


def _rope(x, cos_t, sin_t, half, group):
    lane = lax.broadcasted_iota(I32, x.shape, 1) % group
    partner = jnp.where(lane < half, pltpu.roll(x, LANES - half, 1), pltpu.roll(x, half, 1))
    return x * cos_t + partner * sin_t


VT_ONES = 16
VT_ROWS = LANES + VT_ONES
M_INIT = -1e20


def _store_vt(vt_ref, head, v, stage_ref):
    r0 = head * VT_ROWS
    stage_ref[...] = v
    vt_ref[0, r0:r0 + LANES, :] = stage_ref[...].T.astype(BF)
    vt_ref[0, r0 + LANES:r0 + VT_ROWS, :] = jnp.ones((VT_ONES, v.shape[0]), BF)


LOG2_E = 1.4426950408889634


def _softmax_step(s_t, m, acc, vt):
    m_new = jnp.maximum(m, jnp.max(s_t, axis=0, keepdims=True))
    p = jnp.exp2(s_t - m_new).astype(BF)
    acc = jnp.exp2(m - m_new) * acc + _dot(vt, p)
    return m_new, acc


def _softmax_finish(acc):
    return (acc[:LANES] / acc[LANES:LANES + 1]).T


def _ada_kernel(c_ref, w_ref, b_ref, o_ref):
    c = c_ref[...]
    cond = c / (1.0 + jnp.exp(-c))
    o_ref[0] = _dot(cond.astype(BF), w_ref[0].astype(BF)) + b_ref[0]


def _ada(c, ada_w, ada_b):
    depth, d, n = ada_w.shape
    rows = 8
    c_pad = jnp.zeros((rows, d), F32).at[: c.shape[0]].set(c)
    tn = 1536
    out = pl.pallas_call(
        _ada_kernel,
        name="ada_mod",
        out_shape=jax.ShapeDtypeStruct((depth, rows, n), F32),
        grid=(depth, n // tn),
        in_specs=[
            pl.BlockSpec((rows, d), lambda l, j: (0, 0)),
            pl.BlockSpec((1, d, tn), lambda l, j: (l, 0, j)),
            pl.BlockSpec((1, 1, tn), lambda l, j: (l, 0, j)),
        ],
        out_specs=pl.BlockSpec((1, rows, tn), lambda l, j: (l, 0, j)),
        compiler_params=_params(("parallel", "parallel")),
    )(c_pad, ada_w, ada_b.reshape(depth, 1, n))
    return out[:, : c.shape[0]]


def _out_kernel(o_ref, w_ref, x_ref, g_ref, gate_ref, xo_ref):
    y = _dot(o_ref[0], w_ref[...])
    xo_ref[0] = x_ref[0] + gate_ref[0] * _rms(y, g_ref[...])


def _out_proj(o, w, x, g, gate, tm=512):
    b, s, d = x.shape
    k = o.shape[-1]
    row = lambda bi, i: (bi, i, 0)
    return pl.pallas_call(
        _out_kernel,
        name="out_proj",
        out_shape=jax.ShapeDtypeStruct(x.shape, F32),
        grid=(b, s // tm),
        in_specs=[
            pl.BlockSpec((1, tm, k), row),
            pl.BlockSpec((k, d), lambda bi, i: (0, 0)),
            pl.BlockSpec((1, tm, d), row),
            pl.BlockSpec((1, d), lambda bi, i: (0, 0)),
            pl.BlockSpec((1, 1, d), lambda bi, i: (bi, 0, 0)),
        ],
        out_specs=pl.BlockSpec((1, tm, d), row),
        compiler_params=_params(("parallel", "parallel")),
    )(o, w, x, g, gate)


def _sb_in_kernel(x_ref, g_ref, sc_ref, sh_ref, w_ref, o_ref, *, tn, n_q):
    h = _norm_mod(x_ref[0], g_ref[...], sc_ref[0], sh_ref[0]).astype(BF)
    scale = SB_HEAD_DIM ** -0.5
    for j in range(0, w_ref.shape[1], tn):
        y = _dot(h, w_ref[:, j:j + tn])
        if j < n_q:
            y = y * scale
        o_ref[0, :, j:j + tn] = y.astype(BF)


def _sb_in(x, g, sc, sh, w, tm=512):
    b, s, d = x.shape
    n = w.shape[1]
    row = lambda bi, i: (bi, i, 0)
    vec = lambda bi, i: (bi, 0, 0)
    return pl.pallas_call(
        functools.partial(_sb_in_kernel, tn=512, n_q=SB_HEADS * SB_HEAD_DIM),
        name="sb_in",
        out_shape=jax.ShapeDtypeStruct((b, s, n), BF),
        grid=(b, s // tm),
        in_specs=[
            pl.BlockSpec((1, tm, d), row),
            pl.BlockSpec((1, d), lambda bi, i: (0, 0)),
            pl.BlockSpec((1, 1, d), vec),
            pl.BlockSpec((1, 1, d), vec),
            pl.BlockSpec((d, n), lambda bi, i: (0, 0)),
        ],
        out_specs=pl.BlockSpec((1, tm, n), row),
        compiler_params=_params(("parallel", "parallel")),
    )(x, g, sc, sh, w)


def _sb_attn_kernel(q_ref, k_ref, v_ref, o_ref, *, t, heads):
    i = pl.program_id(2)
    d = SB_HEAD_DIM
    r_io = lax.broadcasted_iota(I32, (t, t + LANES), 0)
    c_io = lax.broadcasted_iota(I32, (t, t + LANES), 1)
    after = jnp.logical_or(r_io > c_io, c_io >= t).astype(BF)
    mask = lax.broadcasted_iota(I32, (t, t), 1) < lax.broadcasted_iota(I32, (t, t), 0)

    def block(j, hd, run, acc, diagonal):
        start = pl.multiple_of(j * t, t)
        cols = slice(hd * d, (hd + 1) * d)
        z = _dot_t(q_ref[0, :, cols], k_ref[0, pl.ds(start, t), cols])
        sp = jnp.log1p(jnp.exp(-jnp.abs(z)))
        log_beta = jnp.minimum(z, 0.0) - sp
        log_fail = -jnp.maximum(z, 0.0) - sp
        if diagonal:
            log_fail = jnp.where(mask, log_fail, 0.0)
        hi = log_fail.astype(BF)
        lo = (log_fail - hi.astype(F32)).astype(BF)
        sums = _dot(hi, after) + _dot(lo, after)
        tail = jnp.concatenate(
            [sums[:, u * LANES:(u + 1) * LANES] + run for u in range(t // LANES)], axis=1)
        a = jnp.exp(log_beta + tail)
        if diagonal:
            a = jnp.where(mask, a, 0.0)
        acc = acc + _dot(a.astype(BF), v_ref[0, pl.ds(start, t), cols])
        return run + sums[:, t:], acc

    def blocks(j, state, diagonal):
        out = []
        for hd in range(heads):
            out.extend(block(j, hd, state[2 * hd], state[2 * hd + 1], diagonal))
        return tuple(out)

    def live(state):
        top = state[0]
        for hd in range(1, heads):
            top = jnp.maximum(top, state[2 * hd])
        return jnp.max(top) > SB_UNDERFLOW

    state = blocks(i, (jnp.zeros((t, LANES), F32), jnp.zeros((t, d), F32)) * heads, True)

    def cond(carry):
        return jnp.logical_and(carry[0] >= 0, carry[1])

    def body(carry):
        state = blocks(carry[0], carry[2:], False)
        return (carry[0] - 1, live(state)) + state

    res = lax.while_loop(cond, body, (i - 1, live(state)) + state)
    for hd in range(heads):
        o_ref[0, :, hd * d:(hd + 1) * d] = res[2 + 2 * hd + 1].astype(BF)


def _sb_attn(qkv, t=256, heads=4):
    b, s, _ = qkv.shape
    ng = SB_HEADS // heads
    w = heads * SB_HEAD_DIM
    return pl.pallas_call(
        functools.partial(_sb_attn_kernel, t=t, heads=heads),
        name="sb_attn",
        out_shape=jax.ShapeDtypeStruct((b, s, SB_HEADS * SB_HEAD_DIM), BF),
        grid=(b, ng, s // t),
        in_specs=[
            pl.BlockSpec((1, t, w), lambda bi, h, i: (bi, i, h)),
            pl.BlockSpec((1, s, w), lambda bi, h, i: (bi, 0, ng + h)),
            pl.BlockSpec((1, s, w), lambda bi, h, i: (bi, 0, 2 * ng + h)),
        ],
        out_specs=pl.BlockSpec((1, t, w), lambda bi, h, i: (bi, i, h)),
        compiler_params=_params(("parallel", "parallel", "arbitrary")),
    )(qkv, qkv, qkv)


def _mla_in_kernel(x_ref, g_ref, sc_ref, sh_ref, w_in_ref, gq_ref, wq_ref, gkv_ref, wkv_ref,
                   cos_ref, sin_ref, q_ref, k_ref, v_ref, stage_ref):
    h = _norm_mod(x_ref[0], g_ref[...], sc_ref[0], sh_ref[0]).astype(BF)
    lat = _dot(h, w_in_ref[...])
    cos_t, sin_t = cos_ref[0], sin_ref[0]
    half = MLA_ROPE_DIM // 2
    c_q = _rms(lat[:, :MLA_Q_LORA], gq_ref[...]).astype(BF)
    c_kv = _rms(lat[:, MLA_Q_LORA:MLA_Q_LORA + MLA_KV_LORA], gkv_ref[...]).astype(BF)
    k_rope = _rope(lat[:, MLA_Q_LORA + MLA_KV_LORA:], cos_t, sin_t, half, LANES).astype(BF)
    scale = (MLA_NOPE_DIM + MLA_ROPE_DIM) ** -0.5 * LOG2_E
    for hd in range(MLA_HEADS):
        c0 = hd * MLA_QK_PAD
        qh = _dot(c_q, wq_ref[:, c0:c0 + MLA_QK_PAD])
        q_ref[0, :, c0:c0 + LANES] = (qh[:, :LANES] * scale).astype(BF)
        q_ref[0, :, c0 + LANES:c0 + MLA_QK_PAD] = (
            _rope(qh[:, LANES:], cos_t, sin_t, half, LANES) * scale).astype(BF)
        k_ref[0, :, c0:c0 + LANES] = _dot(c_kv, wkv_ref[:, hd * LANES:(hd + 1) * LANES]).astype(BF)
        k_ref[0, :, c0 + LANES:c0 + MLA_QK_PAD] = k_rope
    nk = MLA_HEADS * MLA_NOPE_DIM
    for hd in range(MLA_HEADS):
        v = _dot(c_kv, wkv_ref[:, nk + hd * LANES:nk + (hd + 1) * LANES])
        _store_vt(v_ref, hd, v, stage_ref)


def _mla_in(x, g, sc, sh, w_in, g_q, w_q, g_kv, w_kv, cos_t, sin_t, tm=512):
    b, s, d = x.shape
    row = lambda bi, i: (bi, i, 0)
    vec = lambda bi, i: (bi, 0, 0)
    full = lambda bi, i: (0, 0)
    nqk = MLA_HEADS * MLA_QK_PAD
    nv = MLA_HEADS * VT_ROWS
    return pl.pallas_call(
        _mla_in_kernel,
        name="mla_in",
        out_shape=(jax.ShapeDtypeStruct((b, s, nqk), BF), jax.ShapeDtypeStruct((b, s, nqk), BF),
                   jax.ShapeDtypeStruct((b, nv, s), BF)),
        grid=(b, s // tm),
        in_specs=[
            pl.BlockSpec((1, tm, d), row),
            pl.BlockSpec((1, d), full),
            pl.BlockSpec((1, 1, d), vec),
            pl.BlockSpec((1, 1, d), vec),
            pl.BlockSpec(w_in.shape, full),
            pl.BlockSpec(g_q.shape, full),
            pl.BlockSpec(w_q.shape, full),
            pl.BlockSpec(g_kv.shape, full),
            pl.BlockSpec(w_kv.shape, full),
            pl.BlockSpec((1, tm, LANES), row),
            pl.BlockSpec((1, tm, LANES), row),
        ],
        out_specs=(pl.BlockSpec((1, tm, nqk), row), pl.BlockSpec((1, tm, nqk), row),
                   pl.BlockSpec((1, nv, tm), lambda bi, i: (bi, 0, i))),
        scratch_shapes=[pltpu.VMEM((tm, LANES), F32)],
        compiler_params=_params(("parallel", "parallel")),
    )(x, g, sc, sh, w_in, g_q, w_q, g_kv, w_kv, cos_t, sin_t)


def _flash_kernel(q_ref, k_ref, vt_ref, o_ref, *, t, per_trip):
    i = pl.program_id(2)
    q = q_ref[0]

    def scores(j):
        start = pl.multiple_of(j * t, t)
        return _dot_t(k_ref[0, pl.ds(start, t), :], q)

    def step(j, s_t, m, acc):
        start = pl.multiple_of(j * t, t)
        return _softmax_step(s_t, m, acc, vt_ref[0, :, pl.ds(start, t)])

    def run_blocks(first, last, width, carry):
        def trip(jj, carry):
            j0 = first + width * jj
            s_all = [scores(j0 + u) for u in range(width)]
            for u in range(width):
                carry = step(j0 + u, s_all[u], *carry)
            return carry
        n_trips = (last - first) // width
        return lax.fori_loop(0, n_trips, trip, carry), first + width * n_trips

    carry = (jnp.full((1, t), M_INIT, F32), jnp.zeros((VT_ROWS, t), F32))
    first = 0
    width = per_trip
    while width >= 1:
        carry, first = run_blocks(first, i, width, carry)
        width //= 2
    key_io = lax.broadcasted_iota(I32, (t, t), 0)
    qry_io = lax.broadcasted_iota(I32, (t, t), 1)
    m, acc = step(i, jnp.where(key_io <= qry_io, scores(i), NEG_BIG), *carry)
    o_ref[0] = _softmax_finish(acc).astype(BF)


def _flash(q, k, vt, heads, dk, t=512):
    b, s, _ = q.shape
    return pl.pallas_call(
        functools.partial(_flash_kernel, t=t, per_trip=4),
        name="flash_attn",
        out_shape=jax.ShapeDtypeStruct((b, s, heads * LANES), BF),
        grid=(b, heads, s // t),
        in_specs=[
            pl.BlockSpec((1, t, dk), lambda bi, h, i: (bi, i, h)),
            pl.BlockSpec((1, s, dk), lambda bi, h, i: (bi, 0, h)),
            pl.BlockSpec((1, VT_ROWS, s), lambda bi, h, i: (bi, h, 0)),
        ],
        out_specs=pl.BlockSpec((1, t, LANES), lambda bi, h, i: (bi, i, h)),
        compiler_params=_params(("parallel", "parallel", "arbitrary")),
    )(q, k, vt)


INT_MAX = 2 ** 31 - 1
VALUE_STEPS = 24


def _float_to_key(x):
    bits = pltpu.bitcast(x, I32)
    return jnp.where(bits < 0, bits ^ 0x7FFFFFFF, bits)


def _key_to_float(key):
    return pltpu.bitcast(jnp.where(key < 0, key ^ 0x7FFFFFFF, key), F32)


DSA_NQ = DSA_HEADS * DSA_HEAD_DIM
DSA_NKV = DSA_KV_HEADS * DSA_HEAD_DIM
DSA_NQI = IDX_HEADS * IDX_DIM
DSA_COLS = DSA_NQ + 2 * DSA_NKV + DSA_NQI + 2 * LANES


def _dsa_in_kernel(x_ref, g_ref, sc_ref, sh_ref, w_ref, gk_ref, cos_a_ref, sin_a_ref, cos_i_ref,
                   sin_i_ref, q_ref, k_ref, v_ref, qi_ref, ki_ref, wi_ref, stage_ref):
    h = _norm_mod(x_ref[0], g_ref[...], sc_ref[0], sh_ref[0]).astype(BF)
    cos_a, sin_a = cos_a_ref[0], sin_a_ref[0]
    cos_i, sin_i = cos_i_ref[0], sin_i_ref[0]
    scale = DSA_HEAD_DIM ** -0.5 * LOG2_E
    ha, hi = DSA_ROT_DIM // 2, IDX_ROT_DIM // 2
    for hd in range(DSA_HEADS):
        c0 = hd * LANES
        y = _dot(h, w_ref[:, c0:c0 + LANES])
        q_ref[0, :, c0:c0 + LANES] = (_rope(y, cos_a, sin_a, ha, LANES) * scale).astype(BF)
    for hd in range(DSA_KV_HEADS):
        c0 = hd * LANES
        y = _dot(h, w_ref[:, DSA_NQ + c0:DSA_NQ + c0 + LANES])
        k_ref[0, :, c0:c0 + LANES] = _rope(y, cos_a, sin_a, ha, LANES).astype(BF)
    off = DSA_NQ + DSA_NKV
    for hd in range(DSA_KV_HEADS):
        _store_vt(v_ref, hd, _dot(h, w_ref[:, off + hd * LANES:off + (hd + 1) * LANES]), stage_ref)
    off += DSA_NKV
    for p in range(DSA_NQI // LANES):
        c0 = p * LANES
        y = _dot(h, w_ref[:, off + c0:off + c0 + LANES])
        qi_ref[0, :, c0:c0 + LANES] = _rope(y, cos_i, sin_i, hi, IDX_DIM).astype(BF)
    off += DSA_NQI
    y = _dot(h, w_ref[:, off:off + LANES])
    y = y * lax.rsqrt(jnp.sum(y * y, axis=-1, keepdims=True) * (1.0 / IDX_DIM) + EPS) * gk_ref[...]
    y = _rope(y, cos_i, sin_i, hi, IDX_DIM)
    ki_ref[0] = (y + pltpu.roll(y, IDX_DIM, 1)).astype(BF)
    off += LANES
    wi_ref[0] = _dot(h, w_ref[:, off:off + LANES]) * (IDX_HEADS ** -0.5 * IDX_DIM ** -0.5)


def _dsa_in(x, g, sc, sh, w, gk, cos_a, sin_a, cos_i, sin_i, tm=512):
    b, s, d = x.shape
    row = lambda bi, i: (bi, i, 0)
    vec = lambda bi, i: (bi, 0, 0)
    full = lambda bi, i: (0, 0)
    tab = pl.BlockSpec((1, tm, LANES), row)
    nvt = DSA_KV_HEADS * VT_ROWS

    def spec(n):
        return pl.BlockSpec((1, tm, n), row)

    def sds(n, dt=BF):
        return jax.ShapeDtypeStruct((b, s, n), dt)

    return pl.pallas_call(
        _dsa_in_kernel,
        name="dsa_in",
        out_shape=(sds(DSA_NQ), sds(DSA_NKV), jax.ShapeDtypeStruct((b, nvt, s), BF), sds(DSA_NQI),
                   sds(LANES), sds(LANES, F32)),
        grid=(b, s // tm),
        in_specs=[
            pl.BlockSpec((1, tm, d), row),
            pl.BlockSpec((1, d), full),
            pl.BlockSpec((1, 1, d), vec),
            pl.BlockSpec((1, 1, d), vec),
            pl.BlockSpec(w.shape, full),
            pl.BlockSpec(gk.shape, full),
            tab, tab, tab, tab,
        ],
        out_specs=(spec(DSA_NQ), spec(DSA_NKV), pl.BlockSpec((1, nvt, tm), lambda bi, i: (bi, 0, i)),
                   spec(DSA_NQI), spec(LANES), spec(LANES)),
        scratch_shapes=[pltpu.VMEM((tm, LANES), F32)],
        compiler_params=_params(("parallel", "parallel")),
    )(x, g, sc, sh, w, gk, cos_a, sin_a, cos_i, sin_i)


def _dsa_attn_kernel(q_ref, qi_ref, wi_ref, ki_ref, k_ref, vt_ref, o_ref, key_scr, *, tq, tk, n_sel):
    i = pl.program_id(1)
    n_chunks = (i * tq + tq + tk - 1) // tk
    group = DSA_HEADS // DSA_KV_HEADS
    key_io = lax.broadcasted_iota(I32, (tk, tq), 0)
    qry = lax.broadcasted_iota(I32, (1, tq), 1) + i * tq
    lane = lax.broadcasted_iota(I32, (tq, LANES), 1)

    qi = qi_ref[0]
    zero = jnp.zeros_like(qi[:, :LANES])
    stacked = []
    for hd in range(IDX_HEADS):
        pair = qi[:, (hd // 2) * LANES:(hd // 2 + 1) * LANES]
        low = (hd % 2) == 0
        stacked.append(jnp.where((lane < IDX_DIM) == low, pair, zero))
    q_all = jnp.concatenate(stacked, axis=0)
    w_t = wi_ref[0].T

    def fold8(x, op):
        for shift in (4, 2, 1):
            x = op(x, pltpu.roll(x, shift, 0))
        return x

    def score_chunk(c, carry):
        k_min, k_max = carry
        start = pl.multiple_of(c * tk, tk)
        logits = _dot_t(ki_ref[0, pl.ds(start, tk), :], q_all)
        score = jnp.zeros((tk, tq), F32)
        for hd in range(IDX_HEADS):
            score = score + jnp.maximum(logits[:, hd * tq:(hd + 1) * tq], 0.0) * w_t[hd:hd + 1]
        key = _float_to_key(score)
        key = jnp.where(score == 0.0, 0, key)
        causal = key_io + start <= qry
        key_scr[pl.ds(start, tk), :] = jnp.where(causal, key, INT_MIN)
        k_min = jnp.minimum(k_min, jnp.min(
            jnp.where(causal, key, INT_MAX).reshape(tk // 8, 8, tq), axis=0))
        k_max = jnp.maximum(k_max, jnp.max(
            jnp.where(causal, key, INT_MIN).reshape(tk // 8, 8, tq), axis=0))
        return k_min, k_max

    k_min, k_max = lax.fori_loop(
        0, n_chunks, score_chunk,
        (jnp.full((8, tq), INT_MAX, I32), jnp.full((8, tq), INT_MIN, I32)))
    k_min, k_max = fold8(k_min, jnp.minimum), fold8(k_max, jnp.maximum)

    tc = min(tk, 512)

    def count(pred):
        def chunk(c, cnt):
            start = pl.multiple_of(c * tc, tc)
            key = key_scr[pl.ds(start, tc), :].reshape(tc // 8, 8, tq)
            krow = (lax.broadcasted_iota(I32, (tc, tq), 0) + start).reshape(tc // 8, 8, tq)
            return cnt + jnp.sum(pred(key, krow).astype(I32), axis=0)
        cnt = lax.fori_loop(0, n_chunks * (tk // tc), chunk, jnp.zeros((8, tq), I32))
        return fold8(cnt, jnp.add)

    def search_step(it, lo, hi, n_lo, n_hi, done):
        f_lo, f_hi = _key_to_float(lo), _key_to_float(hi)
        frac = ((n_lo - n_sel).astype(F32) - 0.5) / (n_lo - n_hi).astype(F32)
        by_count = _float_to_key(f_lo + frac * (f_hi - f_lo))
        by_value = _float_to_key(0.5 * f_lo + 0.5 * f_hi)
        by_key = (lo >> 1) + (hi >> 1) + (lo & hi & 1)
        cand = jnp.where(it >= VALUE_STEPS, by_key, jnp.where(it % 2 == 0, by_count, by_value))
        cand = jnp.where(it < 2, it, cand)
        cand = jnp.minimum(jnp.maximum(cand, lo + 1), hi - 1)
        n_c = count(lambda key, krow: key >= cand)
        live = done == 0
        up = jnp.logical_and(live, n_c >= n_sel)
        down = jnp.logical_and(live, n_c < n_sel)
        lo, n_lo = jnp.where(up, cand, lo), jnp.where(up, n_c, n_lo)
        hi, n_hi = jnp.where(down, cand, hi), jnp.where(down, n_c, n_hi)
        done = jnp.where(jnp.logical_or(n_lo == n_sel, hi == lo + 1), 1, done)
        return lo, hi, n_lo, n_hi, done

    def search_cond(carry):
        return jnp.logical_and(carry[0] < VALUE_STEPS + 34, jnp.min(carry[-1]) == 0)

    def search_body(carry):
        it, state = carry[0], carry[1:]
        for u in range(4):
            state = search_step(it + u, *state)
        return (it + 4,) + state

    n_lo0 = jnp.zeros((8, tq), I32) + (qry + 1)
    hi0 = k_max + 1
    done0 = jnp.where(jnp.logical_or(n_lo0 <= n_sel, hi0 == k_min + 1), 1, 0)
    res = lax.while_loop(search_cond, search_body,
                         (jnp.int32(0), k_min, hi0, n_lo0, jnp.zeros((8, tq), I32), done0))
    thr, n_at = res[1], res[3]
    tie_bits = key_scr.shape[0].bit_length()

    def break_ties(_):
        want = n_sel - count(lambda key, krow: key > thr)

        def cut_bit(b, cut):
            cand = cut + lax.shift_left(jnp.int32(1), tie_bits - 1 - b)
            n_tie = count(lambda key, krow: jnp.logical_and(key == thr, krow < cand))
            return jnp.where(n_tie <= want, cand, cut)
        cut = lax.fori_loop(0, tie_bits, cut_bit, jnp.zeros((8, tq), I32))[:1]

        def demote(c, _):
            start = pl.multiple_of(c * tc, tc)
            key = key_scr[pl.ds(start, tc), :]
            krow = lax.broadcasted_iota(I32, (tc, tq), 0) + start
            drop = jnp.logical_and(key == thr[:1], krow >= cut)
            key_scr[pl.ds(start, tc), :] = jnp.where(drop, key - 1, key)
            return 0
        lax.fori_loop(0, n_chunks * (tk // tc), demote, 0)
        return jnp.int32(0)

    crowded = jnp.max(jnp.where(jnp.logical_and(n_at > n_sel, thr > INT_MIN), 1, 0)) > 0
    lax.cond(crowded, break_ties, lambda _: jnp.int32(0), 0)
    thr = thr[:1]

    q = q_ref[0]
    q_g = [jnp.concatenate([q[:, (g * group + n) * LANES:(g * group + n + 1) * LANES]
                            for n in range(group)], axis=0) for g in range(DSA_KV_HEADS)]

    def scores(c, g):
        start = pl.multiple_of(c * tk, tk)
        return _dot_t(k_ref[0, pl.ds(start, tk), g * LANES:(g + 1) * LANES], q_g[g])

    def attn_chunk(c, carry):
        start = pl.multiple_of(c * tk, tk)
        s_t = [scores(c, g) for g in range(DSA_KV_HEADS)]
        bias = jnp.where(key_scr[pl.ds(start, tk), :] >= thr, 0.0, NEG_BIG)
        bias = jnp.concatenate([bias] * group, axis=1)
        out = []
        for g in range(DSA_KV_HEADS):
            m, acc = carry[2 * g:2 * g + 2]
            out.extend(_softmax_step(
                s_t[g] + bias, m, acc, vt_ref[0, g * VT_ROWS:(g + 1) * VT_ROWS, pl.ds(start, tk)]))
        return tuple(out)

    init = (jnp.full((1, group * tq), M_INIT, F32), jnp.zeros((VT_ROWS, group * tq), F32))
    res = lax.fori_loop(0, n_chunks, attn_chunk, init * DSA_KV_HEADS)
    for g in range(DSA_KV_HEADS):
        acc = res[2 * g + 1]
        for n in range(group):
            hd = g * group + n
            o_ref[0, :, hd * LANES:(hd + 1) * LANES] = _softmax_finish(
                acc[:, n * tq:(n + 1) * tq]).astype(BF)


def _dsa_attn(q, qi, wi, ki, k, vt, tq=128, tk=1024):
    b, s, _ = q.shape
    n_sel = min(DSA_TOPK_MAX, s // 4)
    tk = min(tk, s)
    row = lambda bi, i: (bi, i, 0)
    whole = lambda bi, i: (bi, 0, 0)
    return pl.pallas_call(
        functools.partial(_dsa_attn_kernel, tq=tq, tk=tk, n_sel=n_sel),
        name="dsa_attn",
        out_shape=jax.ShapeDtypeStruct((b, s, DSA_NQ), BF),
        grid=(b, s // tq),
        in_specs=[
            pl.BlockSpec((1, tq, DSA_NQ), row),
            pl.BlockSpec((1, tq, DSA_NQI), row),
            pl.BlockSpec((1, tq, LANES), row),
            pl.BlockSpec((1, s, LANES), whole),
            pl.BlockSpec((1, s, DSA_NKV), whole),
            pl.BlockSpec((1, DSA_KV_HEADS * VT_ROWS, s), whole),
        ],
        out_specs=pl.BlockSpec((1, tq, DSA_NQ), row),
        scratch_shapes=[pltpu.VMEM((s, tq), I32)],
        compiler_params=_params(("parallel", "arbitrary")),
    )(q, qi, wi, ki, k, vt)


def _ffn_kernel(x_ref, g_ref, sc_ref, sh_ref, wg_ref, wu_ref, wd_ref, gp_ref, gate_ref, xo_ref,
                h_scr, acc_scr):
    e = pl.program_id(2)

    @pl.when(e == 0)
    def _():
        h_scr[...] = _norm_mod(x_ref[0], g_ref[...], sc_ref[0], sh_ref[0]).astype(BF)
        acc_scr[...] = jnp.zeros(acc_scr.shape, F32)

    h = h_scr[...]
    a = _dot(h, wg_ref[0])
    u = _dot(h, wu_ref[0])
    act = (a / (1.0 + jnp.exp(-a)) * u).astype(BF)
    acc_scr[...] += _dot(act, wd_ref[0])

    @pl.when(e == pl.num_programs(2) - 1)
    def _():
        xo_ref[0] = x_ref[0] + gate_ref[0] * _rms(acc_scr[...], gp_ref[...])


def _ffn(x, g, sc, sh, w_gate, w_up, w_down, g_post, gate, tm=512):
    b, s, d = x.shape
    n_e, _, f = w_gate.shape
    row = lambda bi, i, e: (bi, i, 0)
    vec = lambda bi, i, e: (bi, 0, 0)
    full = lambda bi, i, e: (0, 0)
    slab = lambda bi, i, e: (e, 0, 0)
    return pl.pallas_call(
        _ffn_kernel,
        name="ffn",
        out_shape=jax.ShapeDtypeStruct(x.shape, F32),
        grid=(b, s // tm, n_e),
        in_specs=[
            pl.BlockSpec((1, tm, d), row),
            pl.BlockSpec((1, d), full),
            pl.BlockSpec((1, 1, d), vec),
            pl.BlockSpec((1, 1, d), vec),
            pl.BlockSpec((1, d, f), slab),
            pl.BlockSpec((1, d, f), slab),
            pl.BlockSpec((1, f, d), slab),
            pl.BlockSpec((1, d), full),
            pl.BlockSpec((1, 1, d), vec),
        ],
        out_specs=pl.BlockSpec((1, tm, d), row),
        scratch_shapes=[pltpu.VMEM((tm, d), BF), pltpu.VMEM((tm, d), F32)],
        compiler_params=_params(("parallel", "parallel", "arbitrary")),
    )(x, g, sc, sh, w_gate, w_up, w_down, g_post, gate)


MOE_TILE = 1024
MOE_ROWS = 256


def _moe_kernel(x_ref, g_ref, sc_ref, sh_ref, wr_ref, br_ref, wg_ref, wu_ref, wd_ref, gp_ref,
                gate_ref, xo_ref, h_scr, gate_c, rank_c, rank_r, acc_scr, *, rows):
    e = pl.program_id(2)
    t = h_scr.shape[0]
    lane = lax.broadcasted_iota(I32, (t, LANES), 1)

    @pl.when(e == 0)
    def _():
        h32 = _norm_mod(x_ref[0], g_ref[...], sc_ref[0], sh_ref[0])
        h = h32.astype(BF)
        h_scr[...] = h
        acc_scr[...] = jnp.zeros(acc_scr.shape, F32)
        h_lo = (h32 - h.astype(F32)).astype(BF)
        logits = _dot(h, wr_ref[0]) + (_dot(h, wr_ref[1]) + _dot(h_lo, wr_ref[0]))
        logits = jnp.where(lane < N_EXPERTS, logits + br_ref[...], -jnp.inf)
        m1 = jnp.max(logits, axis=-1, keepdims=True)
        i1 = jnp.min(jnp.where(logits == m1, lane, LANES), axis=-1, keepdims=True)
        rest = jnp.where(lane == i1, -jnp.inf, logits)
        m2 = jnp.max(rest, axis=-1, keepdims=True)
        i2 = jnp.min(jnp.where(rest == m2, lane, LANES), axis=-1, keepdims=True)
        e2 = jnp.exp(m2 - m1)
        w1 = 1.0 / (1.0 + e2)
        gate_c[...] = jnp.where(lane == i1, w1, 0.0) + jnp.where(lane == i2, e2 * w1, 0.0)
        sel = jnp.logical_or(lane == i1, lane == i2)
        earlier = (lax.broadcasted_iota(I32, (t, t), 1) < lax.broadcasted_iota(I32, (t, t), 0))
        rank = _dot(earlier.astype(BF), jnp.where(sel, 1.0, 0.0).astype(BF))
        rank = jnp.where(sel, rank, -1.0)
        rank_c[...] = rank
        rank_r[...] = rank.T

    rank_row = rank_r[pl.ds(e, 1), :]
    on_e = lane == e
    rank_col = jnp.sum(jnp.where(on_e, rank_c[...], 0.0), axis=-1, keepdims=True)
    gate_col = jnp.sum(jnp.where(on_e, gate_c[...], 0.0), axis=-1, keepdims=True)
    n_tokens = jnp.max(rank_row).astype(I32) + 1

    def expert_pass(first_row, rows):
        base = first_row.astype(F32)
        row_io = lax.broadcasted_iota(I32, (rows, t), 0).astype(F32) + base
        pick = jnp.where(rank_row == row_io, 1.0, 0.0).astype(BF)
        hs = _dot(pick, h_scr[...]).astype(BF)
        a = _dot(hs, wg_ref[0])
        u = _dot(hs, wu_ref[0])
        act = (a / (1.0 + jnp.exp(-a)) * u).astype(BF)
        y = _dot(act, wd_ref[0]).astype(BF)
        col_io = lax.broadcasted_iota(I32, (t, rows), 1).astype(F32) + base
        place = jnp.where(rank_col == col_io, 1.0, 0.0).astype(BF)
        acc_scr[...] += gate_col * _dot(place, y)

    half = rows // 2
    n_half = (n_tokens + half - 1) // half

    def full_pass(p, _):
        expert_pass(p * rows, rows)
        return 0

    lax.fori_loop(0, n_half // 2, full_pass, 0)

    @pl.when(n_half % 2 == 1)
    def _():
        expert_pass((n_half // 2) * rows, half)

    @pl.when(e == pl.num_programs(2) - 1)
    def _():
        xo_ref[0] = x_ref[0] + gate_ref[0] * _rms(acc_scr[...], gp_ref[...])


def _moe(x, g, sc, sh, w_router, b_router, w_gate, w_up, w_down, g_post, gate):
    b, s, d = x.shape
    n_e, _, f = w_gate.shape
    tm = min(MOE_TILE, s)
    row = lambda bi, i, e: (bi, i, 0)
    vec = lambda bi, i, e: (bi, 0, 0)
    full = lambda bi, i, e: (0, 0)
    slab = lambda bi, i, e: (e, 0, 0)
    once = pl.Buffered(1)
    return pl.pallas_call(
        functools.partial(_moe_kernel, rows=min(MOE_ROWS, tm)),
        name="moe",
        out_shape=jax.ShapeDtypeStruct(x.shape, F32),
        grid=(b, s // tm, n_e),
        in_specs=[
            pl.BlockSpec((1, tm, d), row, pipeline_mode=once),
            pl.BlockSpec((1, d), full),
            pl.BlockSpec((1, 1, d), vec),
            pl.BlockSpec((1, 1, d), vec),
            pl.BlockSpec((2, d, LANES), lambda bi, i, e: (0, 0, 0)),
            pl.BlockSpec((1, LANES), full),
            pl.BlockSpec((1, d, f), slab),
            pl.BlockSpec((1, d, f), slab),
            pl.BlockSpec((1, f, d), slab),
            pl.BlockSpec((1, d), full),
            pl.BlockSpec((1, 1, d), vec),
        ],
        out_specs=pl.BlockSpec((1, tm, d), row, pipeline_mode=once),
        scratch_shapes=[
            pltpu.VMEM((tm, d), BF),
            pltpu.VMEM((tm, LANES), F32),
            pltpu.VMEM((tm, LANES), F32),
            pltpu.VMEM((LANES, tm), F32),
            pltpu.VMEM((tm, d), F32),
        ],
        compiler_params=_params(("parallel", "parallel", "arbitrary")),
    )(x, g, sc, sh, w_router, b_router, w_gate, w_up, w_down, g_post, gate)


def _rope_tables(positions, rot_dim, group):
    half = rot_dim // 2
    inv_freq = ROPE_THETA ** (-jnp.arange(0, rot_dim, 2, dtype=F32) / rot_dim)
    ang = positions.astype(F32)[..., None] * inv_freq
    cos, sin = jnp.cos(ang), jnp.sin(ang)
    pad = group - rot_dim
    ones = jnp.ones(cos.shape[:-1] + (pad,), F32)
    zeros = jnp.zeros(cos.shape[:-1] + (pad,), F32)
    cos_t = jnp.concatenate([cos, cos, ones], axis=-1)
    sin_t = jnp.concatenate([-sin, sin, zeros], axis=-1)
    reps = LANES // group
    return jnp.tile(cos_t, (1, 1, reps)), jnp.tile(sin_t, (1, 1, reps))


def _pad_cols(w, n):
    return jnp.pad(w, ((0, 0), (0, n - w.shape[1])))


def _split_gu(w_gu, f):
    return w_gu[..., :f].astype(BF), w_gu[..., f:].astype(BF)


def kernel(x, c, positions, ada_w, ada_b, norm_g, sb_w_in, sb_w_out, mla_w_in, mla_g_q, mla_w_q_up,
           mla_g_kv, mla_w_kv_up, mla_w_out, dsa_w_in, dsa_g_kidx, dsa_w_out, ffn_w_gu, ffn_w_down,
           moe_w_router, moe_b_router, moe_w_gu, moe_w_down):
    depth = ada_w.shape[0]
    b, s, d = x.shape
    mod = _ada(c, ada_w, ada_b)
    cos_m, sin_m = _rope_tables(positions, MLA_ROPE_DIM, LANES)
    cos_a, sin_a = _rope_tables(positions, DSA_ROT_DIM, LANES)
    cos_i, sin_i = _rope_tables(positions, IDX_ROT_DIM, IDX_DIM)
    counters = [0, 0, 0]
    for layer in range(depth):
        sh_m, sc_m, g_m, sh_f, sc_f, g_f = [
            mod[layer, :, n * d:(n + 1) * d].reshape(b, 1, d) for n in range(6)]
        gn = norm_g[layer].reshape(4, 1, d)
        kind = layer % N_MIXERS
        j = counters[kind]
        counters[kind] += 1
        if kind == 0:
            qkv = _sb_in(x, gn[0], sc_m, sh_m, sb_w_in[j].astype(BF))
            o = _sb_attn(qkv)
            w_out = sb_w_out[j]
        elif kind == 1:
            w_in = _pad_cols(mla_w_in[j], 4 * LANES).astype(BF)
            w_q = mla_w_q_up[j].reshape(MLA_Q_LORA, MLA_HEADS, MLA_NOPE_DIM + MLA_ROPE_DIM)
            w_q = jnp.pad(w_q, ((0, 0), (0, 0), (0, MLA_QK_PAD - w_q.shape[-1])))
            w_q = w_q.reshape(MLA_Q_LORA, MLA_HEADS * MLA_QK_PAD).astype(BF)
            w_kv = mla_w_kv_up[j].reshape(MLA_KV_LORA, MLA_HEADS, MLA_NOPE_DIM + MLA_V_DIM)
            w_kv = jnp.concatenate([w_kv[..., :MLA_NOPE_DIM].reshape(MLA_KV_LORA, -1),
                                    w_kv[..., MLA_NOPE_DIM:].reshape(MLA_KV_LORA, -1)], axis=1)
            q, k, v = _mla_in(x, gn[0], sc_m, sh_m, w_in, mla_g_q[j].reshape(1, -1), w_q,
                              mla_g_kv[j].reshape(1, -1), w_kv.astype(BF), cos_m, sin_m)
            o = _flash(q, k, v, MLA_HEADS, MLA_QK_PAD)
            w_out = mla_w_out[j]
        else:
            w = dsa_w_in[j]
            n_main = DSA_NQ + 2 * DSA_NKV + DSA_NQI
            w = jnp.concatenate([w[:, :n_main], _pad_cols(w[:, n_main:n_main + IDX_DIM], LANES),
                                 _pad_cols(w[:, n_main + IDX_DIM:], LANES)], axis=1).astype(BF)
            gk = _pad_cols(dsa_g_kidx[j].reshape(1, -1), LANES)
            q, k, v, qi, ki, wi = _dsa_in(x, gn[0], sc_m, sh_m, w, gk, cos_a, sin_a, cos_i, sin_i)
            o = _dsa_attn(q, qi, wi, ki, k, v)
            w_out = dsa_w_out[j]
        x = _out_proj(o, w_out.astype(BF), x, gn[1], g_m)
        f = layer // 2
        if layer % 2 == 0:
            half = D_FF // 2
            w_gate, w_up = _split_gu(ffn_w_gu[f], D_FF)
            w_gate = w_gate.reshape(d, 2, half).transpose(1, 0, 2)
            w_up = w_up.reshape(d, 2, half).transpose(1, 0, 2)
            w_down = ffn_w_down[f].astype(BF).reshape(2, half, d)
            x = _ffn(x, gn[2], sc_f, sh_f, w_gate, w_up, w_down, gn[3], g_f)
        else:
            w_gate, w_up = _split_gu(moe_w_gu[f], D_FF_EXPERT)
            w_r = _pad_cols(moe_w_router[f], LANES)
            w_r_hi = w_r.astype(BF)
            w_r = jnp.stack([w_r_hi, (w_r - w_r_hi.astype(F32)).astype(BF)])
            b_r = _pad_cols(moe_b_router[f].reshape(1, -1), LANES)
            x = _moe(x, gn[2], sc_f, sh_f, w_r, b_r, w_gate, w_up, moe_w_down[f].astype(BF), gn[3],
                     g_f)
    return x
```

```python
import functools

import jax
import jax.numpy as jnp
from jax import lax
from jax.experimental import pallas as pl
from jax.experimental.pallas import tpu as pltpu

BF = jnp.bfloat16
F32 = jnp.float32
I32 = jnp.int32

D_MODEL = 1024
N_MIXERS = 3
ROPE_THETA = 500000.0
EPS = 1e-6
LANES = 128

SB_HEADS = 8
SB_HEAD_DIM = 128
MLA_HEADS = 8
MLA_Q_LORA = 256
MLA_KV_LORA = 128
MLA_NOPE_DIM = 128
MLA_ROPE_DIM = 64
MLA_V_DIM = 128
MLA_QK_PAD = 256
DSA_HEADS = 8
DSA_KV_HEADS = 2
DSA_HEAD_DIM = 128
DSA_ROT_DIM = 32
IDX_HEADS = 8
IDX_DIM = 64
IDX_ROT_DIM = 16
DSA_TOPK_MAX = 256
D_FF = 2816
N_EXPERTS = 8
D_FF_EXPERT = 1408

VMEM_LIMIT = 56 * 1024 * 1024
INT_MIN = -(2 ** 31)
SB_UNDERFLOW = -104.0
NEG_BIG = -1e30


def _params(sem):
    return pltpu.CompilerParams(dimension_semantics=sem, vmem_limit_bytes=VMEM_LIMIT)


def _dot(a, b):
    return jnp.dot(a, b, preferred_element_type=F32)


def _dot_t(a, b):
    return lax.dot_general(a, b, (((1,), (1,)), ((), ())), preferred_element_type=F32)


def _rms(x, g):
    return x * lax.rsqrt(jnp.mean(x * x, axis=-1, keepdims=True) + EPS) * g


def _norm_mod(x, g, sc, sh):
    return _rms(x, g) * (1.0 + sc) + sh


def _rope(x, cos_t, sin_t, half, group):
    lane = lax.broadcasted_iota(I32, x.shape, 1) % group
    partner = jnp.where(lane < half, pltpu.roll(x, LANES - half, 1), pltpu.roll(x, half, 1))
    return x * cos_t + partner * sin_t


VT_ONES = 16
VT_ROWS = LANES + VT_ONES
M_INIT = -1e20


def _store_vt(vt_ref, head, v, stage_ref):
    r0 = head * VT_ROWS
    stage_ref[...] = v
    vt_ref[0, r0:r0 + LANES, :] = stage_ref[...].T.astype(BF)
    vt_ref[0, r0 + LANES:r0 + VT_ROWS, :] = jnp.ones((VT_ONES, v.shape[0]), BF)


LOG2_E = 1.4426950408889634


def _softmax_step(s_t, m, acc, vt):
    m_new = jnp.maximum(m, jnp.max(s_t, axis=0, keepdims=True))
    p = jnp.exp2(s_t - m_new).astype(BF)
    acc = jnp.exp2(m - m_new) * acc + _dot(vt, p)
    return m_new, acc


def _softmax_finish(acc):
    return (acc[:LANES] / acc[LANES:LANES + 1]).T


def _ada_kernel(c_ref, w_ref, b_ref, o_ref):
    c = c_ref[...]
    cond = c / (1.0 + jnp.exp(-c))
    o_ref[0] = _dot(cond.astype(BF), w_ref[0].astype(BF)) + b_ref[0]


def _ada(c, ada_w, ada_b):
    depth, d, n = ada_w.shape
    rows = 8
    c_pad = jnp.zeros((rows, d), F32).at[: c.shape[0]].set(c)
    tn = 1536
    out = pl.pallas_call(
        _ada_kernel,
        name="ada_mod",
        out_shape=jax.ShapeDtypeStruct((depth, rows, n), F32),
        grid=(depth, n // tn),
        in_specs=[
            pl.BlockSpec((rows, d), lambda l, j: (0, 0)),
            pl.BlockSpec((1, d, tn), lambda l, j: (l, 0, j)),
            pl.BlockSpec((1, 1, tn), lambda l, j: (l, 0, j)),
        ],
        out_specs=pl.BlockSpec((1, rows, tn), lambda l, j: (l, 0, j)),
        compiler_params=_params(("parallel", "parallel")),
    )(c_pad, ada_w, ada_b.reshape(depth, 1, n))
    return out[:, : c.shape[0]]


def _out_kernel(o_ref, w_ref, x_ref, g_ref, gate_ref, xo_ref):
    y = _dot(o_ref[0], w_ref[...])
    xo_ref[0] = x_ref[0] + gate_ref[0] * _rms(y, g_ref[...])


def _out_proj(o, w, x, g, gate, tm=512):
    b, s, d = x.shape
    k = o.shape[-1]
    row = lambda bi, i: (bi, i, 0)
    return pl.pallas_call(
        _out_kernel,
        name="out_proj",
        out_shape=jax.ShapeDtypeStruct(x.shape, F32),
        grid=(b, s // tm),
        in_specs=[
            pl.BlockSpec((1, tm, k), row),
            pl.BlockSpec((k, d), lambda bi, i: (0, 0)),
            pl.BlockSpec((1, tm, d), row),
            pl.BlockSpec((1, d), lambda bi, i: (0, 0)),
            pl.BlockSpec((1, 1, d), lambda bi, i: (bi, 0, 0)),
        ],
        out_specs=pl.BlockSpec((1, tm, d), row),
        compiler_params=_params(("parallel", "parallel")),
    )(o, w, x, g, gate)


def _sb_in_kernel(x_ref, g_ref, sc_ref, sh_ref, w_ref, o_ref, *, tn, n_q):
    h = _norm_mod(x_ref[0], g_ref[...], sc_ref[0], sh_ref[0]).astype(BF)
    scale = SB_HEAD_DIM ** -0.5
    for j in range(0, w_ref.shape[1], tn):
        y = _dot(h, w_ref[:, j:j + tn])
        if j < n_q:
            y = y * scale
        o_ref[0, :, j:j + tn] = y.astype(BF)


def _sb_in(x, g, sc, sh, w, tm=512):
    b, s, d = x.shape
    n = w.shape[1]
    row = lambda bi, i: (bi, i, 0)
    vec = lambda bi, i: (bi, 0, 0)
    return pl.pallas_call(
        functools.partial(_sb_in_kernel, tn=512, n_q=SB_HEADS * SB_HEAD_DIM),
        name="sb_in",
        out_shape=jax.ShapeDtypeStruct((b, s, n), BF),
        grid=(b, s // tm),
        in_specs=[
            pl.BlockSpec((1, tm, d), row),
            pl.BlockSpec((1, d), lambda bi, i: (0, 0)),
            pl.BlockSpec((1, 1, d), vec),
            pl.BlockSpec((1, 1, d), vec),
            pl.BlockSpec((d, n), lambda bi, i: (0, 0)),
        ],
        out_specs=pl.BlockSpec((1, tm, n), row),
        compiler_params=_params(("parallel", "parallel")),
    )(x, g, sc, sh, w)


def _sb_attn_kernel(q_ref, k_ref, v_ref, o_ref, *, t, heads):
    i = pl.program_id(2)
    d = SB_HEAD_DIM
    r_io = lax.broadcasted_iota(I32, (t, t + LANES), 0)
    c_io = lax.broadcasted_iota(I32, (t, t + LANES), 1)
    after = jnp.logical_or(r_io > c_io, c_io >= t).astype(BF)
    mask = lax.broadcasted_iota(I32, (t, t), 1) < lax.broadcasted_iota(I32, (t, t), 0)

    def block(j, hd, run, acc, diagonal):
        start = pl.multiple_of(j * t, t)
        cols = slice(hd * d, (hd + 1) * d)
        z = _dot_t(q_ref[0, :, cols], k_ref[0, pl.ds(start, t), cols])
        sp = jnp.log(1.0 + jnp.exp(-jnp.abs(z)))
        log_beta = jnp.minimum(z, 0.0) - sp
        log_fail = -jnp.maximum(z, 0.0) - sp
        if diagonal:
            log_fail = jnp.where(mask, log_fail, 0.0)
        hi = log_fail.astype(BF)
        lo = (log_fail - hi.astype(F32)).astype(BF)
        sums = _dot(hi, after) + _dot(lo, after)
        tail = jnp.concatenate(
            [sums[:, u * LANES:(u + 1) * LANES] + run for u in range(t // LANES)], axis=1)
        a = jnp.exp(log_beta + tail)
        if diagonal:
            a = jnp.where(mask, a, 0.0)
        acc = acc + _dot(a.astype(BF), v_ref[0, pl.ds(start, t), cols])
        return run + sums[:, t:], acc

    def blocks(j, state, diagonal):
        out = []
        for hd in range(heads):
            out.extend(block(j, hd, state[2 * hd], state[2 * hd + 1], diagonal))
        return tuple(out)

    def live(state):
        top = state[0]
        for hd in range(1, heads):
            top = jnp.maximum(top, state[2 * hd])
        return jnp.max(top) > SB_UNDERFLOW

    state = blocks(i, (jnp.zeros((t, LANES), F32), jnp.zeros((t, d), F32)) * heads, True)

    def cond(carry):
        return jnp.logical_and(carry[0] >= 0, carry[1])

    def body(carry):
        state = blocks(carry[0], carry[2:], False)
        return (carry[0] - 1, live(state)) + state

    res = lax.while_loop(cond, body, (i - 1, live(state)) + state)
    for hd in range(heads):
        o_ref[0, :, hd * d:(hd + 1) * d] = res[2 + 2 * hd + 1].astype(BF)


def _sb_attn(qkv, t=256, heads=4):
    b, s, _ = qkv.shape
    ng = SB_HEADS // heads
    w = heads * SB_HEAD_DIM
    return pl.pallas_call(
        functools.partial(_sb_attn_kernel, t=t, heads=heads),
        name="sb_attn",
        out_shape=jax.ShapeDtypeStruct((b, s, SB_HEADS * SB_HEAD_DIM), BF),
        grid=(b, ng, s // t),
        in_specs=[
            pl.BlockSpec((1, t, w), lambda bi, h, i: (bi, i, h)),
            pl.BlockSpec((1, s, w), lambda bi, h, i: (bi, 0, ng + h)),
            pl.BlockSpec((1, s, w), lambda bi, h, i: (bi, 0, 2 * ng + h)),
        ],
        out_specs=pl.BlockSpec((1, t, w), lambda bi, h, i: (bi, i, h)),
        compiler_params=_params(("parallel", "parallel", "arbitrary")),
    )(qkv, qkv, qkv)


def _mla_in_kernel(x_ref, g_ref, sc_ref, sh_ref, w_in_ref, gq_ref, wq_ref, gkv_ref, wkv_ref,
                   cos_ref, sin_ref, q_ref, k_ref, v_ref, stage_ref):
    h = _norm_mod(x_ref[0], g_ref[...], sc_ref[0], sh_ref[0]).astype(BF)
    lat = _dot(h, w_in_ref[...])
    cos_t, sin_t = cos_ref[0], sin_ref[0]
    half = MLA_ROPE_DIM // 2
    c_q = _rms(lat[:, :MLA_Q_LORA], gq_ref[...]).astype(BF)
    c_kv = _rms(lat[:, MLA_Q_LORA:MLA_Q_LORA + MLA_KV_LORA], gkv_ref[...]).astype(BF)
    k_rope = _rope(lat[:, MLA_Q_LORA + MLA_KV_LORA:], cos_t, sin_t, half, LANES).astype(BF)
    scale = (MLA_NOPE_DIM + MLA_ROPE_DIM) ** -0.5 * LOG2_E
    for hd in range(MLA_HEADS):
        c0 = hd * MLA_QK_PAD
        qh = _dot(c_q, wq_ref[:, c0:c0 + MLA_QK_PAD])
        q_ref[0, :, c0:c0 + LANES] = (qh[:, :LANES] * scale).astype(BF)
        q_ref[0, :, c0 + LANES:c0 + MLA_QK_PAD] = (
            _rope(qh[:, LANES:], cos_t, sin_t, half, LANES) * scale).astype(BF)
        k_ref[0, :, c0:c0 + LANES] = _dot(c_kv, wkv_ref[:, hd * LANES:(hd + 1) * LANES]).astype(BF)
        k_ref[0, :, c0 + LANES:c0 + MLA_QK_PAD] = k_rope
    nk = MLA_HEADS * MLA_NOPE_DIM
    for hd in range(MLA_HEADS):
        v = _dot(c_kv, wkv_ref[:, nk + hd * LANES:nk + (hd + 1) * LANES])
        _store_vt(v_ref, hd, v, stage_ref)


def _mla_in(x, g, sc, sh, w_in, g_q, w_q, g_kv, w_kv, cos_t, sin_t, tm=512):
    b, s, d = x.shape
    row = lambda bi, i: (bi, i, 0)
    vec = lambda bi, i: (bi, 0, 0)
    full = lambda bi, i: (0, 0)
    nqk = MLA_HEADS * MLA_QK_PAD
    nv = MLA_HEADS * VT_ROWS
    return pl.pallas_call(
        _mla_in_kernel,
        name="mla_in",
        out_shape=(jax.ShapeDtypeStruct((b, s, nqk), BF), jax.ShapeDtypeStruct((b, s, nqk), BF),
                   jax.ShapeDtypeStruct((b, nv, s), BF)),
        grid=(b, s // tm),
        in_specs=[
            pl.BlockSpec((1, tm, d), row),
            pl.BlockSpec((1, d), full),
            pl.BlockSpec((1, 1, d), vec),
            pl.BlockSpec((1, 1, d), vec),
            pl.BlockSpec(w_in.shape, full),
            pl.BlockSpec(g_q.shape, full),
            pl.BlockSpec(w_q.shape, full),
            pl.BlockSpec(g_kv.shape, full),
            pl.BlockSpec(w_kv.shape, full),
            pl.BlockSpec((1, tm, LANES), row),
            pl.BlockSpec((1, tm, LANES), row),
        ],
        out_specs=(pl.BlockSpec((1, tm, nqk), row), pl.BlockSpec((1, tm, nqk), row),
                   pl.BlockSpec((1, nv, tm), lambda bi, i: (bi, 0, i))),
        scratch_shapes=[pltpu.VMEM((tm, LANES), F32)],
        compiler_params=_params(("parallel", "parallel")),
    )(x, g, sc, sh, w_in, g_q, w_q, g_kv, w_kv, cos_t, sin_t)


def _flash_kernel(q_ref, k_ref, vt_ref, o_ref, *, t, per_trip):
    i = pl.program_id(2)
    q = q_ref[0]

    def scores(j):
        start = pl.multiple_of(j * t, t)
        return _dot_t(k_ref[0, pl.ds(start, t), :], q)

    def step(j, s_t, m, acc):
        start = pl.multiple_of(j * t, t)
        return _softmax_step(s_t, m, acc, vt_ref[0, :, pl.ds(start, t)])

    def run_blocks(first, last, width, carry):
        def trip(jj, carry):
            j0 = first + width * jj
            s_all = [scores(j0 + u) for u in range(width)]
            for u in range(width):
                carry = step(j0 + u, s_all[u], *carry)
            return carry
        n_trips = (last - first) // width
        return lax.fori_loop(0, n_trips, trip, carry), first + width * n_trips

    carry = (jnp.full((1, t), M_INIT, F32), jnp.zeros((VT_ROWS, t), F32))
    first = 0
    width = per_trip
    while width >= 1:
        carry, first = run_blocks(first, i, width, carry)
        width //= 2
    key_io = lax.broadcasted_iota(I32, (t, t), 0)
    qry_io = lax.broadcasted_iota(I32, (t, t), 1)
    m, acc = step(i, jnp.where(key_io <= qry_io, scores(i), NEG_BIG), *carry)
    o_ref[0] = _softmax_finish(acc).astype(BF)


def _flash(q, k, vt, heads, dk, t=512):
    b, s, _ = q.shape
    return pl.pallas_call(
        functools.partial(_flash_kernel, t=t, per_trip=4),
        name="flash_attn",
        out_shape=jax.ShapeDtypeStruct((b, s, heads * LANES), BF),
        grid=(b, heads, s // t),
        in_specs=[
            pl.BlockSpec((1, t, dk), lambda bi, h, i: (bi, i, h)),
            pl.BlockSpec((1, s, dk), lambda bi, h, i: (bi, 0, h)),
            pl.BlockSpec((1, VT_ROWS, s), lambda bi, h, i: (bi, h, 0)),
        ],
        out_specs=pl.BlockSpec((1, t, LANES), lambda bi, h, i: (bi, i, h)),
        compiler_params=_params(("parallel", "parallel", "arbitrary")),
    )(q, k, vt)


INT_MAX = 2 ** 31 - 1
VALUE_STEPS = 24


def _float_to_key(x):
    bits = pltpu.bitcast(x, I32)
    return jnp.where(bits < 0, bits ^ 0x7FFFFFFF, bits)


def _key_to_float(key):
    return pltpu.bitcast(jnp.where(key < 0, key ^ 0x7FFFFFFF, key), F32)


DSA_NQ = DSA_HEADS * DSA_HEAD_DIM
DSA_NKV = DSA_KV_HEADS * DSA_HEAD_DIM
DSA_NQI = IDX_HEADS * IDX_DIM
DSA_COLS = DSA_NQ + 2 * DSA_NKV + DSA_NQI + 2 * LANES


def _dsa_in_kernel(x_ref, g_ref, sc_ref, sh_ref, w_ref, gk_ref, cos_a_ref, sin_a_ref, cos_i_ref,
                   sin_i_ref, q_ref, k_ref, v_ref, qi_ref, ki_ref, wi_ref, stage_ref):
    h = _norm_mod(x_ref[0], g_ref[...], sc_ref[0], sh_ref[0]).astype(BF)
    cos_a, sin_a = cos_a_ref[0], sin_a_ref[0]
    cos_i, sin_i = cos_i_ref[0], sin_i_ref[0]
    scale = DSA_HEAD_DIM ** -0.5 * LOG2_E
    ha, hi = DSA_ROT_DIM // 2, IDX_ROT_DIM // 2
    for hd in range(DSA_HEADS):
        c0 = hd * LANES
        y = _dot(h, w_ref[:, c0:c0 + LANES])
        q_ref[0, :, c0:c0 + LANES] = (_rope(y, cos_a, sin_a, ha, LANES) * scale).astype(BF)
    for hd in range(DSA_KV_HEADS):
        c0 = hd * LANES
        y = _dot(h, w_ref[:, DSA_NQ + c0:DSA_NQ + c0 + LANES])
        k_ref[0, :, c0:c0 + LANES] = _rope(y, cos_a, sin_a, ha, LANES).astype(BF)
    off = DSA_NQ + DSA_NKV
    for hd in range(DSA_KV_HEADS):
        _store_vt(v_ref, hd, _dot(h, w_ref[:, off + hd * LANES:off + (hd + 1) * LANES]), stage_ref)
    off += DSA_NKV
    for p in range(DSA_NQI // LANES):
        c0 = p * LANES
        y = _dot(h, w_ref[:, off + c0:off + c0 + LANES])
        qi_ref[0, :, c0:c0 + LANES] = _rope(y, cos_i, sin_i, hi, IDX_DIM).astype(BF)
    off += DSA_NQI
    y = _dot(h, w_ref[:, off:off + LANES])
    y = y * lax.rsqrt(jnp.sum(y * y, axis=-1, keepdims=True) * (1.0 / IDX_DIM) + EPS) * gk_ref[...]
    y = _rope(y, cos_i, sin_i, hi, IDX_DIM)
    ki_ref[0] = (y + pltpu.roll(y, IDX_DIM, 1)).astype(BF)
    off += LANES
    wi_ref[0] = _dot(h, w_ref[:, off:off + LANES]) * (IDX_HEADS ** -0.5 * IDX_DIM ** -0.5)


def _dsa_in(x, g, sc, sh, w, gk, cos_a, sin_a, cos_i, sin_i, tm=512):
    b, s, d = x.shape
    row = lambda bi, i: (bi, i, 0)
    vec = lambda bi, i: (bi, 0, 0)
    full = lambda bi, i: (0, 0)
    tab = pl.BlockSpec((1, tm, LANES), row)
    nvt = DSA_KV_HEADS * VT_ROWS

    def spec(n):
        return pl.BlockSpec((1, tm, n), row)

    def sds(n, dt=BF):
        return jax.ShapeDtypeStruct((b, s, n), dt)

    return pl.pallas_call(
        _dsa_in_kernel,
        name="dsa_in",
        out_shape=(sds(DSA_NQ), sds(DSA_NKV), jax.ShapeDtypeStruct((b, nvt, s), BF), sds(DSA_NQI),
                   sds(LANES), sds(LANES, F32)),
        grid=(b, s // tm),
        in_specs=[
            pl.BlockSpec((1, tm, d), row),
            pl.BlockSpec((1, d), full),
            pl.BlockSpec((1, 1, d), vec),
            pl.BlockSpec((1, 1, d), vec),
            pl.BlockSpec(w.shape, full),
            pl.BlockSpec(gk.shape, full),
            tab, tab, tab, tab,
        ],
        out_specs=(spec(DSA_NQ), spec(DSA_NKV), pl.BlockSpec((1, nvt, tm), lambda bi, i: (bi, 0, i)),
                   spec(DSA_NQI), spec(LANES), spec(LANES)),
        scratch_shapes=[pltpu.VMEM((tm, LANES), F32)],
        compiler_params=_params(("parallel", "parallel")),
    )(x, g, sc, sh, w, gk, cos_a, sin_a, cos_i, sin_i)


def _dsa_attn_kernel(q_ref, qi_ref, wi_ref, ki_ref, k_ref, vt_ref, o_ref, key_scr, *, tq, tk, n_sel):
    i = pl.program_id(1)
    n_chunks = (i * tq + tq + tk - 1) // tk
    group = DSA_HEADS // DSA_KV_HEADS
    key_io = lax.broadcasted_iota(I32, (tk, tq), 0)
    qry = lax.broadcasted_iota(I32, (1, tq), 1) + i * tq
    lane = lax.broadcasted_iota(I32, (tq, LANES), 1)

    qi = qi_ref[0]
    zero = jnp.zeros_like(qi[:, :LANES])
    stacked = []
    for hd in range(IDX_HEADS):
        pair = qi[:, (hd // 2) * LANES:(hd // 2 + 1) * LANES]
        low = (hd % 2) == 0
        stacked.append(jnp.where((lane < IDX_DIM) == low, pair, zero))
    q_all = jnp.concatenate(stacked, axis=0)
    w_t = wi_ref[0].T

    def fold8(x, op):
        for shift in (4, 2, 1):
            x = op(x, pltpu.roll(x, shift, 0))
        return x

    def score_chunk(c, carry):
        k_min, k_max = carry
        start = pl.multiple_of(c * tk, tk)
        logits = _dot_t(ki_ref[0, pl.ds(start, tk), :], q_all)
        score = jnp.zeros((tk, tq), F32)
        for hd in range(IDX_HEADS):
            score = score + jnp.maximum(logits[:, hd * tq:(hd + 1) * tq], 0.0) * w_t[hd:hd + 1]
        key = _float_to_key(score)
        key = jnp.where(score == 0.0, 0, key)
        causal = key_io + start <= qry
        key_scr[pl.ds(start, tk), :] = jnp.where(causal, key, INT_MIN)
        k_min = jnp.minimum(k_min, jnp.min(
            jnp.where(causal, key, INT_MAX).reshape(tk // 8, 8, tq), axis=0))
        k_max = jnp.maximum(k_max, jnp.max(
            jnp.where(causal, key, INT_MIN).reshape(tk // 8, 8, tq), axis=0))
        return k_min, k_max

    k_min, k_max = lax.fori_loop(
        0, n_chunks, score_chunk,
        (jnp.full((8, tq), INT_MAX, I32), jnp.full((8, tq), INT_MIN, I32)))
    k_min, k_max = fold8(k_min, jnp.minimum), fold8(k_max, jnp.maximum)

    tc = min(tk, 512)
    n_pieces = (i * tq + tq + tc - 1) // tc

    def count(pred):
        def chunk(c, cnt):
            start = pl.multiple_of(c * tc, tc)
            key = key_scr[pl.ds(start, tc), :].reshape(tc // 8, 8, tq)
            krow = (lax.broadcasted_iota(I32, (tc, tq), 0) + start).reshape(tc // 8, 8, tq)
            return cnt + jnp.sum(pred(key, krow).astype(I32), axis=0)
        cnt = lax.fori_loop(0, n_pieces, chunk, jnp.zeros((8, tq), I32))
        return fold8(cnt, jnp.add)

    def search_step(it, lo, hi, n_lo, n_hi, done):
        f_lo, f_hi = _key_to_float(lo), _key_to_float(hi)
        frac = ((n_lo - n_sel).astype(F32) - 0.5) / (n_lo - n_hi).astype(F32)
        by_count = _float_to_key(f_lo + frac * (f_hi - f_lo))
        by_value = _float_to_key(0.5 * f_lo + 0.5 * f_hi)
        by_key = (lo >> 1) + (hi >> 1) + (lo & hi & 1)
        cand = jnp.where(it >= VALUE_STEPS, by_key, jnp.where(it % 2 == 0, by_count, by_value))
        cand = jnp.where(it < 2, it, cand)
        cand = jnp.minimum(jnp.maximum(cand, lo + 1), hi - 1)
        n_c = count(lambda key, krow: key >= cand)
        live = done == 0
        up = jnp.logical_and(live, n_c >= n_sel)
        down = jnp.logical_and(live, n_c < n_sel)
        lo, n_lo = jnp.where(up, cand, lo), jnp.where(up, n_c, n_lo)
        hi, n_hi = jnp.where(down, cand, hi), jnp.where(down, n_c, n_hi)
        done = jnp.where(jnp.logical_or(n_lo == n_sel, hi == lo + 1), 1, done)
        return lo, hi, n_lo, n_hi, done

    def search_cond(carry):
        return jnp.logical_and(carry[0] < VALUE_STEPS + 34, jnp.min(carry[-1]) == 0)

    def search_body(carry):
        it, state = carry[0], carry[1:]
        for u in range(4):
            state = search_step(it + u, *state)
        return (it + 4,) + state

    n_lo0 = jnp.zeros((8, tq), I32) + (qry + 1)
    hi0 = k_max + 1
    done0 = jnp.where(jnp.logical_or(n_lo0 <= n_sel, hi0 == k_min + 1), 1, 0)
    res = lax.while_loop(search_cond, search_body,
                         (jnp.int32(0), k_min, hi0, n_lo0, jnp.zeros((8, tq), I32), done0))
    thr, n_at = res[1], res[3]
    tie_bits = key_scr.shape[0].bit_length()

    def break_ties(_):
        want = n_sel - count(lambda key, krow: key > thr)

        def cut_bit(b, cut):
            cand = cut + lax.shift_left(jnp.int32(1), tie_bits - 1 - b)
            n_tie = count(lambda key, krow: jnp.logical_and(key == thr, krow < cand))
            return jnp.where(n_tie <= want, cand, cut)
        cut = lax.fori_loop(0, tie_bits, cut_bit, jnp.zeros((8, tq), I32))[:1]

        def demote(c, _):
            start = pl.multiple_of(c * tc, tc)
            key = key_scr[pl.ds(start, tc), :]
            krow = lax.broadcasted_iota(I32, (tc, tq), 0) + start
            drop = jnp.logical_and(key == thr[:1], krow >= cut)
            key_scr[pl.ds(start, tc), :] = jnp.where(drop, key - 1, key)
            return 0
        lax.fori_loop(0, n_pieces, demote, 0)
        return jnp.int32(0)

    crowded = jnp.max(jnp.where(jnp.logical_and(n_at > n_sel, thr > INT_MIN), 1, 0)) > 0
    lax.cond(crowded, break_ties, lambda _: jnp.int32(0), 0)
    thr = thr[:1]

    q = q_ref[0]
    q_g = [jnp.concatenate([q[:, (g * group + n) * LANES:(g * group + n + 1) * LANES]
                            for n in range(group)], axis=0) for g in range(DSA_KV_HEADS)]

    def scores(c, g):
        start = pl.multiple_of(c * tk, tk)
        return _dot_t(k_ref[0, pl.ds(start, tk), g * LANES:(g + 1) * LANES], q_g[g])

    def attn_chunk(c, carry):
        start = pl.multiple_of(c * tk, tk)
        s_t = [scores(c, g) for g in range(DSA_KV_HEADS)]
        bias = jnp.where(key_scr[pl.ds(start, tk), :] >= thr, 0.0, NEG_BIG)
        bias = jnp.concatenate([bias] * group, axis=1)
        out = []
        for g in range(DSA_KV_HEADS):
            m, acc = carry[2 * g:2 * g + 2]
            out.extend(_softmax_step(
                s_t[g] + bias, m, acc, vt_ref[0, g * VT_ROWS:(g + 1) * VT_ROWS, pl.ds(start, tk)]))
        return tuple(out)

    init = (jnp.full((1, group * tq), M_INIT, F32), jnp.zeros((VT_ROWS, group * tq), F32))
    res = lax.fori_loop(0, n_chunks, attn_chunk, init * DSA_KV_HEADS)
    for g in range(DSA_KV_HEADS):
        acc = res[2 * g + 1]
        for n in range(group):
            hd = g * group + n
            o_ref[0, :, hd * LANES:(hd + 1) * LANES] = _softmax_finish(
                acc[:, n * tq:(n + 1) * tq]).astype(BF)


def _dsa_attn(q, qi, wi, ki, k, vt, tq=128, tk=1024):
    b, s, _ = q.shape
    n_sel = min(DSA_TOPK_MAX, s // 4)
    tk = min(tk, s)
    row = lambda bi, i: (bi, i, 0)
    whole = lambda bi, i: (bi, 0, 0)
    return pl.pallas_call(
        functools.partial(_dsa_attn_kernel, tq=tq, tk=tk, n_sel=n_sel),
        name="dsa_attn",
        out_shape=jax.ShapeDtypeStruct((b, s, DSA_NQ), BF),
        grid=(b, s // tq),
        in_specs=[
            pl.BlockSpec((1, tq, DSA_NQ), row),
            pl.BlockSpec((1, tq, DSA_NQI), row),
            pl.BlockSpec((1, tq, LANES), row),
            pl.BlockSpec((1, s, LANES), whole),
            pl.BlockSpec((1, s, DSA_NKV), whole),
            pl.BlockSpec((1, DSA_KV_HEADS * VT_ROWS, s), whole),
        ],
        out_specs=pl.BlockSpec((1, tq, DSA_NQ), row),
        scratch_shapes=[pltpu.VMEM((s, tq), I32)],
        compiler_params=_params(("parallel", "arbitrary")),
    )(q, qi, wi, ki, k, vt)


def _ffn_kernel(x_ref, g_ref, sc_ref, sh_ref, wg_ref, wu_ref, wd_ref, gp_ref, gate_ref, xo_ref,
                h_scr, acc_scr):
    e = pl.program_id(2)

    @pl.when(e == 0)
    def _():
        h_scr[...] = _norm_mod(x_ref[0], g_ref[...], sc_ref[0], sh_ref[0]).astype(BF)
        acc_scr[...] = jnp.zeros(acc_scr.shape, F32)

    h = h_scr[...]
    a = _dot(h, wg_ref[0])
    u = _dot(h, wu_ref[0])
    act = (a / (1.0 + jnp.exp(-a)) * u).astype(BF)
    acc_scr[...] += _dot(act, wd_ref[0])

    @pl.when(e == pl.num_programs(2) - 1)
    def _():
        xo_ref[0] = x_ref[0] + gate_ref[0] * _rms(acc_scr[...], gp_ref[...])


def _ffn(x, g, sc, sh, w_gate, w_up, w_down, g_post, gate, tm=512):
    b, s, d = x.shape
    n_e, _, f = w_gate.shape
    row = lambda bi, i, e: (bi, i, 0)
    vec = lambda bi, i, e: (bi, 0, 0)
    full = lambda bi, i, e: (0, 0)
    slab = lambda bi, i, e: (e, 0, 0)
    return pl.pallas_call(
        _ffn_kernel,
        name="ffn",
        out_shape=jax.ShapeDtypeStruct(x.shape, F32),
        grid=(b, s // tm, n_e),
        in_specs=[
            pl.BlockSpec((1, tm, d), row),
            pl.BlockSpec((1, d), full),
            pl.BlockSpec((1, 1, d), vec),
            pl.BlockSpec((1, 1, d), vec),
            pl.BlockSpec((1, d, f), slab),
            pl.BlockSpec((1, d, f), slab),
            pl.BlockSpec((1, f, d), slab),
            pl.BlockSpec((1, d), full),
            pl.BlockSpec((1, 1, d), vec),
        ],
        out_specs=pl.BlockSpec((1, tm, d), row),
        scratch_shapes=[pltpu.VMEM((tm, d), BF), pltpu.VMEM((tm, d), F32)],
        compiler_params=_params(("parallel", "parallel", "arbitrary")),
    )(x, g, sc, sh, w_gate, w_up, w_down, g_post, gate)


MOE_TILE = 1024
MOE_ROWS = 256


def _moe_kernel(x_ref, g_ref, sc_ref, sh_ref, wr_ref, br_ref, wg_ref, wu_ref, wd_ref, gp_ref,
                gate_ref, xo_ref, h_scr, gate_c, rank_c, rank_r, acc_scr, *, rows):
    e = pl.program_id(2)
    t = h_scr.shape[0]
    lane = lax.broadcasted_iota(I32, (t, LANES), 1)

    @pl.when(e == 0)
    def _():
        h32 = _norm_mod(x_ref[0], g_ref[...], sc_ref[0], sh_ref[0])
        h = h32.astype(BF)
        h_scr[...] = h
        acc_scr[...] = jnp.zeros(acc_scr.shape, F32)
        h_lo = (h32 - h.astype(F32)).astype(BF)
        logits = _dot(h, wr_ref[0]) + (_dot(h, wr_ref[1]) + _dot(h_lo, wr_ref[0]))
        logits = jnp.where(lane < N_EXPERTS, logits + br_ref[...], -jnp.inf)
        m1 = jnp.max(logits, axis=-1, keepdims=True)
        i1 = jnp.min(jnp.where(logits == m1, lane, LANES), axis=-1, keepdims=True)
        rest = jnp.where(lane == i1, -jnp.inf, logits)
        m2 = jnp.max(rest, axis=-1, keepdims=True)
        i2 = jnp.min(jnp.where(rest == m2, lane, LANES), axis=-1, keepdims=True)
        e2 = jnp.exp(m2 - m1)
        w1 = 1.0 / (1.0 + e2)
        gate_c[...] = jnp.where(lane == i1, w1, 0.0) + jnp.where(lane == i2, e2 * w1, 0.0)
        sel = jnp.logical_or(lane == i1, lane == i2)
        earlier = (lax.broadcasted_iota(I32, (t, t), 1) < lax.broadcasted_iota(I32, (t, t), 0))
        rank = _dot(earlier.astype(BF), jnp.where(sel, 1.0, 0.0).astype(BF))
        rank = jnp.where(sel, rank, -1.0)
        rank_c[...] = rank
        rank_r[...] = rank.T

    rank_row = rank_r[pl.ds(e, 1), :]
    on_e = lane == e
    rank_col = jnp.sum(jnp.where(on_e, rank_c[...], 0.0), axis=-1, keepdims=True)
    gate_col = jnp.sum(jnp.where(on_e, gate_c[...], 0.0), axis=-1, keepdims=True)
    n_tokens = jnp.max(rank_row).astype(I32) + 1

    def expert_pass(first_row, rows):
        base = first_row.astype(F32)
        row_io = lax.broadcasted_iota(I32, (rows, t), 0).astype(F32) + base
        pick = jnp.where(rank_row == row_io, 1.0, 0.0).astype(BF)
        hs = _dot(pick, h_scr[...]).astype(BF)
        a = _dot(hs, wg_ref[0])
        u = _dot(hs, wu_ref[0])
        act = (a / (1.0 + jnp.exp(-a)) * u).astype(BF)
        y = _dot(act, wd_ref[0]).astype(BF)
        col_io = lax.broadcasted_iota(I32, (t, rows), 1).astype(F32) + base
        place = jnp.where(rank_col == col_io, 1.0, 0.0).astype(BF)
        acc_scr[...] += gate_col * _dot(place, y)

    half = rows // 2
    n_half = (n_tokens + half - 1) // half

    def full_pass(p, _):
        expert_pass(p * rows, rows)
        return 0

    lax.fori_loop(0, n_half // 2, full_pass, 0)

    @pl.when(n_half % 2 == 1)
    def _():
        expert_pass((n_half // 2) * rows, half)

    @pl.when(e == pl.num_programs(2) - 1)
    def _():
        xo_ref[0] = x_ref[0] + gate_ref[0] * _rms(acc_scr[...], gp_ref[...])


def _moe(x, g, sc, sh, w_router, b_router, w_gate, w_up, w_down, g_post, gate):
    b, s, d = x.shape
    n_e, _, f = w_gate.shape
    tm = min(MOE_TILE, s)
    row = lambda bi, i, e: (bi, i, 0)
    vec = lambda bi, i, e: (bi, 0, 0)
    full = lambda bi, i, e: (0, 0)
    slab = lambda bi, i, e: (e, 0, 0)
    once = pl.Buffered(1)
    return pl.pallas_call(
        functools.partial(_moe_kernel, rows=min(MOE_ROWS, tm)),
        name="moe",
        out_shape=jax.ShapeDtypeStruct(x.shape, F32),
        grid=(b, s // tm, n_e),
        in_specs=[
            pl.BlockSpec((1, tm, d), row, pipeline_mode=once),
            pl.BlockSpec((1, d), full),
            pl.BlockSpec((1, 1, d), vec),
            pl.BlockSpec((1, 1, d), vec),
            pl.BlockSpec((2, d, LANES), lambda bi, i, e: (0, 0, 0)),
            pl.BlockSpec((1, LANES), full),
            pl.BlockSpec((1, d, f), slab),
            pl.BlockSpec((1, d, f), slab),
            pl.BlockSpec((1, f, d), slab),
            pl.BlockSpec((1, d), full),
            pl.BlockSpec((1, 1, d), vec),
        ],
        out_specs=pl.BlockSpec((1, tm, d), row, pipeline_mode=once),
        scratch_shapes=[
            pltpu.VMEM((tm, d), BF),
            pltpu.VMEM((tm, LANES), F32),
            pltpu.VMEM((tm, LANES), F32),
            pltpu.VMEM((LANES, tm), F32),
            pltpu.VMEM((tm, d), F32),
        ],
        compiler_params=_params(("parallel", "parallel", "arbitrary")),
    )(x, g, sc, sh, w_router, b_router, w_gate, w_up, w_down, g_post, gate)


def _rope_tables(positions, rot_dim, group):
    half = rot_dim // 2
    inv_freq = ROPE_THETA ** (-jnp.arange(0, rot_dim, 2, dtype=F32) / rot_dim)
    ang = positions.astype(F32)[..., None] * inv_freq
    cos, sin = jnp.cos(ang), jnp.sin(ang)
    pad = group - rot_dim
    ones = jnp.ones(cos.shape[:-1] + (pad,), F32)
    zeros = jnp.zeros(cos.shape[:-1] + (pad,), F32)
    cos_t = jnp.concatenate([cos, cos, ones], axis=-1)
    sin_t = jnp.concatenate([-sin, sin, zeros], axis=-1)
    reps = LANES // group
    return jnp.tile(cos_t, (1, 1, reps)), jnp.tile(sin_t, (1, 1, reps))


def _pad_cols(w, n):
    return jnp.pad(w, ((0, 0), (0, n - w.shape[1])))


def _split_gu(w_gu, f):
    return w_gu[..., :f].astype(BF), w_gu[..., f:].astype(BF)


def kernel(x, c, positions, ada_w, ada_b, norm_g, sb_w_in, sb_w_out, mla_w_in, mla_g_q, mla_w_q_up,
           mla_g_kv, mla_w_kv_up, mla_w_out, dsa_w_in, dsa_g_kidx, dsa_w_out, ffn_w_gu, ffn_w_down,
           moe_w_router, moe_b_router, moe_w_gu, moe_w_down):
    depth = ada_w.shape[0]
    b, s, d = x.shape
    mod = _ada(c, ada_w, ada_b)
    cos_m, sin_m = _rope_tables(positions, MLA_ROPE_DIM, LANES)
    cos_a, sin_a = _rope_tables(positions, DSA_ROT_DIM, LANES)
    cos_i, sin_i = _rope_tables(positions, IDX_ROT_DIM, IDX_DIM)
    counters = [0, 0, 0]
    for layer in range(depth):
        sh_m, sc_m, g_m, sh_f, sc_f, g_f = [
            mod[layer, :, n * d:(n + 1) * d].reshape(b, 1, d) for n in range(6)]
        gn = norm_g[layer].reshape(4, 1, d)
        kind = layer % N_MIXERS
        j = counters[kind]
        counters[kind] += 1
        if kind == 0:
            qkv = _sb_in(x, gn[0], sc_m, sh_m, sb_w_in[j].astype(BF))
            o = _sb_attn(qkv)
            w_out = sb_w_out[j]
        elif kind == 1:
            w_in = _pad_cols(mla_w_in[j], 4 * LANES).astype(BF)
            w_q = mla_w_q_up[j].reshape(MLA_Q_LORA, MLA_HEADS, MLA_NOPE_DIM + MLA_ROPE_DIM)
            w_q = jnp.pad(w_q, ((0, 0), (0, 0), (0, MLA_QK_PAD - w_q.shape[-1])))
            w_q = w_q.reshape(MLA_Q_LORA, MLA_HEADS * MLA_QK_PAD).astype(BF)
            w_kv = mla_w_kv_up[j].reshape(MLA_KV_LORA, MLA_HEADS, MLA_NOPE_DIM + MLA_V_DIM)
            w_kv = jnp.concatenate([w_kv[..., :MLA_NOPE_DIM].reshape(MLA_KV_LORA, -1),
                                    w_kv[..., MLA_NOPE_DIM:].reshape(MLA_KV_LORA, -1)], axis=1)
            q, k, v = _mla_in(x, gn[0], sc_m, sh_m, w_in, mla_g_q[j].reshape(1, -1), w_q,
                              mla_g_kv[j].reshape(1, -1), w_kv.astype(BF), cos_m, sin_m)
            o = _flash(q, k, v, MLA_HEADS, MLA_QK_PAD)
            w_out = mla_w_out[j]
        else:
            w = dsa_w_in[j]
            n_main = DSA_NQ + 2 * DSA_NKV + DSA_NQI
            w = jnp.concatenate([w[:, :n_main], _pad_cols(w[:, n_main:n_main + IDX_DIM], LANES),
                                 _pad_cols(w[:, n_main + IDX_DIM:], LANES)], axis=1).astype(BF)
            gk = _pad_cols(dsa_g_kidx[j].reshape(1, -1), LANES)
            q, k, v, qi, ki, wi = _dsa_in(x, gn[0], sc_m, sh_m, w, gk, cos_a, sin_a, cos_i, sin_i)
            o = _dsa_attn(q, qi, wi, ki, k, v)
            w_out = dsa_w_out[j]
        x = _out_proj(o, w_out.astype(BF), x, gn[1], g_m)
        f = layer // 2
        if layer % 2 == 0:
            half = D_FF // 2
            w_gate, w_up = _split_gu(ffn_w_gu[f], D_FF)
            w_gate = w_gate.reshape(d, 2, half).transpose(1, 0, 2)
            w_up = w_up.reshape(d, 2, half).transpose(1, 0, 2)
            w_down = ffn_w_down[f].astype(BF).reshape(2, half, d)
            x = _ffn(x, gn[2], sc_f, sh_f, w_gate, w_up, w_down, gn[3], g_f)
        else:
            w_gate, w_up = _split_gu(moe_w_gu[f], D_FF_EXPERT)
            w_r = _pad_cols(moe_w_router[f], LANES)
            w_r_hi = w_r.astype(BF)
            w_r = jnp.stack([w_r_hi, (w_r - w_r_hi.astype(F32)).astype(BF)])
            b_r = _pad_cols(moe_b_router[f].reshape(1, -1), LANES)
            x = _moe(x, gn[2], sc_f, sh_f, w_r, b_r, w_gate, w_up, moe_w_down[f].astype(BF), gn[3],
                     g_f)
    return x
```

```python
import functools

import jax
import jax.numpy as jnp
from jax import lax
from jax.experimental import pallas as pl
from jax.experimental.pallas import tpu as pltpu

BF = jnp.bfloat16
F32 = jnp.float32
I32 = jnp.int32

D_MODEL = 1024
N_MIXERS = 3
ROPE_THETA = 500000.0
EPS = 1e-6
LANES = 128

SB_HEADS = 8
SB_HEAD_DIM = 128
MLA_HEADS = 8
MLA_Q_LORA = 256
MLA_KV_LORA = 128
MLA_NOPE_DIM = 128
MLA_ROPE_DIM = 64
MLA_V_DIM = 128
MLA_QK_PAD = 256
DSA_HEADS = 8
DSA_KV_HEADS = 2
DSA_HEAD_DIM = 128
DSA_ROT_DIM = 32
IDX_HEADS = 8
IDX_DIM = 64
IDX_ROT_DIM = 16
DSA_TOPK_MAX = 256
D_FF = 2816
N_EXPERTS = 8
D_FF_EXPERT = 1408

VMEM_LIMIT = 56 * 1024 * 1024
INT_MIN = -(2 ** 31)
SB_UNDERFLOW = -104.0
NEG_BIG = -1e30


def _params(sem):
    return pltpu.CompilerParams(dimension_semantics=sem, vmem_limit_bytes=VMEM_LIMIT)


def _dot(a, b):
    return jnp.dot(a, b, preferred_element_type=F32)


def _dot_t(a, b):
    return lax.dot_general(a, b, (((1,), (1,)), ((), ())), preferred_element_type=F32)


def _rms(x, g):
    return x * lax.rsqrt(jnp.mean(x * x, axis=-1, keepdims=True) + EPS) * g


def _norm_mod(x, g, sc, sh):
    return _rms(x, g) * (1.0 + sc) + sh


def _rope(x, cos_t, sin_t, half, group):
    lane = lax.broadcasted_iota(I32, x.shape, 1) % group
    partner = jnp.where(lane < half, pltpu.roll(x, LANES - half, 1), pltpu.roll(x, half, 1))
    return x * cos_t + partner * sin_t


VT_ONES = 16
VT_ROWS = LANES + VT_ONES
M_INIT = -1e20


def _store_vt(vt_ref, head, v, stage_ref):
    r0 = head * VT_ROWS
    stage_ref[...] = v
    vt_ref[0, r0:r0 + LANES, :] = stage_ref[...].T.astype(BF)
    vt_ref[0, r0 + LANES:r0 + VT_ROWS, :] = jnp.ones((VT_ONES, v.shape[0]), BF)


LOG2_E = 1.4426950408889634


def _softmax_step(s_t, m, acc, vt):
    m_new = jnp.maximum(m, jnp.max(s_t, axis=0, keepdims=True))
    p = jnp.exp2(s_t - m_new).astype(BF)
    acc = jnp.exp2(m - m_new) * acc + _dot(vt, p)
    return m_new, acc


def _softmax_finish(acc):
    return (acc[:LANES] / acc[LANES:LANES + 1]).T


def _ada_kernel(c_ref, w_ref, b_ref, o_ref):
    c = c_ref[...]
    cond = c / (1.0 + jnp.exp(-c))
    o_ref[0] = _dot(cond.astype(BF), w_ref[0].astype(BF)) + b_ref[0]


def _ada(c, ada_w, ada_b):
    depth, d, n = ada_w.shape
    rows = 8
    c_pad = jnp.zeros((rows, d), F32).at[: c.shape[0]].set(c)
    tn = 1536
    out = pl.pallas_call(
        _ada_kernel,
        name="ada_mod",
        out_shape=jax.ShapeDtypeStruct((depth, rows, n), F32),
        grid=(depth, n // tn),
        in_specs=[
            pl.BlockSpec((rows, d), lambda l, j: (0, 0)),
            pl.BlockSpec((1, d, tn), lambda l, j: (l, 0, j)),
            pl.BlockSpec((1, 1, tn), lambda l, j: (l, 0, j)),
        ],
        out_specs=pl.BlockSpec((1, rows, tn), lambda l, j: (l, 0, j)),
        compiler_params=_params(("parallel", "parallel")),
    )(c_pad, ada_w, ada_b.reshape(depth, 1, n))
    return out[:, : c.shape[0]]


def _out_kernel(o_ref, w_ref, x_ref, g_ref, gate_ref, xo_ref):
    y = _dot(o_ref[0], w_ref[...])
    xo_ref[0] = x_ref[0] + gate_ref[0] * _rms(y, g_ref[...])


def _out_proj(o, w, x, g, gate, tm=512):
    b, s, d = x.shape
    k = o.shape[-1]
    row = lambda bi, i: (bi, i, 0)
    return pl.pallas_call(
        _out_kernel,
        name="out_proj",
        out_shape=jax.ShapeDtypeStruct(x.shape, F32),
        grid=(b, s // tm),
        in_specs=[
            pl.BlockSpec((1, tm, k), row),
            pl.BlockSpec((k, d), lambda bi, i: (0, 0)),
            pl.BlockSpec((1, tm, d), row),
            pl.BlockSpec((1, d), lambda bi, i: (0, 0)),
            pl.BlockSpec((1, 1, d), lambda bi, i: (bi, 0, 0)),
        ],
        out_specs=pl.BlockSpec((1, tm, d), row),
        compiler_params=_params(("parallel", "parallel")),
    )(o, w, x, g, gate)


def _sb_in_kernel(x_ref, g_ref, sc_ref, sh_ref, w_ref, o_ref, *, tn, n_q):
    h = _norm_mod(x_ref[0], g_ref[...], sc_ref[0], sh_ref[0]).astype(BF)
    scale = SB_HEAD_DIM ** -0.5
    for j in range(0, w_ref.shape[1], tn):
        y = _dot(h, w_ref[:, j:j + tn])
        if j < n_q:
            y = y * scale
        o_ref[0, :, j:j + tn] = y.astype(BF)


def _sb_in(x, g, sc, sh, w, tm=512):
    b, s, d = x.shape
    n = w.shape[1]
    row = lambda bi, i: (bi, i, 0)
    vec = lambda bi, i: (bi, 0, 0)
    return pl.pallas_call(
        functools.partial(_sb_in_kernel, tn=512, n_q=SB_HEADS * SB_HEAD_DIM),
        name="sb_in",
        out_shape=jax.ShapeDtypeStruct((b, s, n), BF),
        grid=(b, s // tm),
        in_specs=[
            pl.BlockSpec((1, tm, d), row),
            pl.BlockSpec((1, d), lambda bi, i: (0, 0)),
            pl.BlockSpec((1, 1, d), vec),
            pl.BlockSpec((1, 1, d), vec),
            pl.BlockSpec((d, n), lambda bi, i: (0, 0)),
        ],
        out_specs=pl.BlockSpec((1, tm, n), row),
        compiler_params=_params(("parallel", "parallel")),
    )(x, g, sc, sh, w)


def _sb_attn_kernel(q_ref, k_ref, v_ref, o_ref, *, t, heads):
    i = pl.program_id(2)
    d = SB_HEAD_DIM
    r_io = lax.broadcasted_iota(I32, (t, t + LANES), 0)
    c_io = lax.broadcasted_iota(I32, (t, t + LANES), 1)
    after = jnp.logical_or(r_io > c_io, c_io >= t).astype(BF)
    mask = lax.broadcasted_iota(I32, (t, t), 1) < lax.broadcasted_iota(I32, (t, t), 0)

    def block(j, hd, run, acc, diagonal):
        start = pl.multiple_of(j * t, t)
        cols = slice(hd * d, (hd + 1) * d)
        z = _dot_t(q_ref[0, :, cols], k_ref[0, pl.ds(start, t), cols])
        sp = jnp.log(1.0 + jnp.exp(-jnp.abs(z)))
        log_beta = jnp.minimum(z, 0.0) - sp
        log_fail = -jnp.maximum(z, 0.0) - sp
        if diagonal:
            log_fail = jnp.where(mask, log_fail, 0.0)
        hi = log_fail.astype(BF)
        lo = (log_fail - hi.astype(F32)).astype(BF)
        sums = _dot(hi, after) + _dot(lo, after)
        tail = jnp.concatenate(
            [sums[:, u * LANES:(u + 1) * LANES] + run for u in range(t // LANES)], axis=1)
        a = jnp.exp(log_beta + tail)
        if diagonal:
            a = jnp.where(mask, a, 0.0)
        acc = acc + _dot(a.astype(BF), v_ref[0, pl.ds(start, t), cols])
        return run + sums[:, t:], acc

    def blocks(j, state, diagonal):
        out = []
        for hd in range(heads):
            out.extend(block(j, hd, state[2 * hd], state[2 * hd + 1], diagonal))
        return tuple(out)

    def live(state):
        top = state[0]
        for hd in range(1, heads):
            top = jnp.maximum(top, state[2 * hd])
        return jnp.max(top) > SB_UNDERFLOW

    state = blocks(i, (jnp.zeros((t, LANES), F32), jnp.zeros((t, d), F32)) * heads, True)

    def cond(carry):
        return jnp.logical_and(carry[0] >= 0, carry[1])

    def body(carry):
        state = blocks(carry[0], carry[2:], False)
        return (carry[0] - 1, live(state)) + state

    res = lax.while_loop(cond, body, (i - 1, live(state)) + state)
    for hd in range(heads):
        o_ref[0, :, hd * d:(hd + 1) * d] = res[2 + 2 * hd + 1].astype(BF)


def _sb_attn(qkv, t=256, heads=4):
    b, s, _ = qkv.shape
    ng = SB_HEADS // heads
    w = heads * SB_HEAD_DIM
    return pl.pallas_call(
        functools.partial(_sb_attn_kernel, t=t, heads=heads),
        name="sb_attn",
        out_shape=jax.ShapeDtypeStruct((b, s, SB_HEADS * SB_HEAD_DIM), BF),
        grid=(b, ng, s // t),
        in_specs=[
            pl.BlockSpec((1, t, w), lambda bi, h, i: (bi, i, h)),
            pl.BlockSpec((1, s, w), lambda bi, h, i: (bi, 0, ng + h)),
            pl.BlockSpec((1, s, w), lambda bi, h, i: (bi, 0, 2 * ng + h)),
        ],
        out_specs=pl.BlockSpec((1, t, w), lambda bi, h, i: (bi, i, h)),
        compiler_params=_params(("parallel", "parallel", "arbitrary")),
    )(qkv, qkv, qkv)


def _mla_in_kernel(x_ref, g_ref, sc_ref, sh_ref, w_in_ref, gq_ref, wq_ref, gkv_ref, wkv_ref,
                   cos_ref, sin_ref, q_ref, k_ref, v_ref, stage_ref):
    h = _norm_mod(x_ref[0], g_ref[...], sc_ref[0], sh_ref[0]).astype(BF)
    lat = _dot(h, w_in_ref[...])
    cos_t, sin_t = cos_ref[0], sin_ref[0]
    half = MLA_ROPE_DIM // 2
    c_q = _rms(lat[:, :MLA_Q_LORA], gq_ref[...]).astype(BF)
    c_kv = _rms(lat[:, MLA_Q_LORA:MLA_Q_LORA + MLA_KV_LORA], gkv_ref[...]).astype(BF)
    k_rope = _rope(lat[:, MLA_Q_LORA + MLA_KV_LORA:], cos_t, sin_t, half, LANES).astype(BF)
    scale = (MLA_NOPE_DIM + MLA_ROPE_DIM) ** -0.5 * LOG2_E
    for hd in range(MLA_HEADS):
        c0 = hd * MLA_QK_PAD
        qh = _dot(c_q, wq_ref[:, c0:c0 + MLA_QK_PAD])
        q_ref[0, :, c0:c0 + LANES] = (qh[:, :LANES] * scale).astype(BF)
        q_ref[0, :, c0 + LANES:c0 + MLA_QK_PAD] = (
            _rope(qh[:, LANES:], cos_t, sin_t, half, LANES) * scale).astype(BF)
        k_ref[0, :, c0:c0 + LANES] = _dot(c_kv, wkv_ref[:, hd * LANES:(hd + 1) * LANES]).astype(BF)
        k_ref[0, :, c0 + LANES:c0 + MLA_QK_PAD] = k_rope
    nk = MLA_HEADS * MLA_NOPE_DIM
    for hd in range(MLA_HEADS):
        v = _dot(c_kv, wkv_ref[:, nk + hd * LANES:nk + (hd + 1) * LANES])
        _store_vt(v_ref, hd, v, stage_ref)


def _mla_in(x, g, sc, sh, w_in, g_q, w_q, g_kv, w_kv, cos_t, sin_t, tm=512):
    b, s, d = x.shape
    row = lambda bi, i: (bi, i, 0)
    vec = lambda bi, i: (bi, 0, 0)
    full = lambda bi, i: (0, 0)
    nqk = MLA_HEADS * MLA_QK_PAD
    nv = MLA_HEADS * VT_ROWS
    return pl.pallas_call(
        _mla_in_kernel,
        name="mla_in",
        out_shape=(jax.ShapeDtypeStruct((b, s, nqk), BF), jax.ShapeDtypeStruct((b, s, nqk), BF),
                   jax.ShapeDtypeStruct((b, nv, s), BF)),
        grid=(b, s // tm),
        in_specs=[
            pl.BlockSpec((1, tm, d), row),
            pl.BlockSpec((1, d), full),
            pl.BlockSpec((1, 1, d), vec),
            pl.BlockSpec((1, 1, d), vec),
            pl.BlockSpec(w_in.shape, full),
            pl.BlockSpec(g_q.shape, full),
            pl.BlockSpec(w_q.shape, full),
            pl.BlockSpec(g_kv.shape, full),
            pl.BlockSpec(w_kv.shape, full),
            pl.BlockSpec((1, tm, LANES), row),
            pl.BlockSpec((1, tm, LANES), row),
        ],
        out_specs=(pl.BlockSpec((1, tm, nqk), row), pl.BlockSpec((1, tm, nqk), row),
                   pl.BlockSpec((1, nv, tm), lambda bi, i: (bi, 0, i))),
        scratch_shapes=[pltpu.VMEM((tm, LANES), F32)],
        compiler_params=_params(("parallel", "parallel")),
    )(x, g, sc, sh, w_in, g_q, w_q, g_kv, w_kv, cos_t, sin_t)


def _flash_kernel(q_ref, k_ref, vt_ref, o_ref, *, t, per_trip):
    i = pl.program_id(2)
    q = q_ref[0]

    def scores(j):
        start = pl.multiple_of(j * t, t)
        return _dot_t(k_ref[0, pl.ds(start, t), :], q)

    def step(j, s_t, m, acc):
        start = pl.multiple_of(j * t, t)
        return _softmax_step(s_t, m, acc, vt_ref[0, :, pl.ds(start, t)])

    def run_blocks(first, last, width, carry):
        def trip(jj, carry):
            j0 = first + width * jj
            s_all = [scores(j0 + u) for u in range(width)]
            for u in range(width):
                carry = step(j0 + u, s_all[u], *carry)
            return carry
        n_trips = (last - first) // width
        return lax.fori_loop(0, n_trips, trip, carry), first + width * n_trips

    carry = (jnp.full((1, t), M_INIT, F32), jnp.zeros((VT_ROWS, t), F32))
    first = 0
    width = per_trip
    while width >= 1:
        carry, first = run_blocks(first, i, width, carry)
        width //= 2
    key_io = lax.broadcasted_iota(I32, (t, t), 0)
    qry_io = lax.broadcasted_iota(I32, (t, t), 1)
    m, acc = step(i, jnp.where(key_io <= qry_io, scores(i), NEG_BIG), *carry)
    o_ref[0] = _softmax_finish(acc).astype(BF)


def _flash(q, k, vt, heads, dk, t=512):
    b, s, _ = q.shape
    return pl.pallas_call(
        functools.partial(_flash_kernel, t=t, per_trip=4),
        name="flash_attn",
        out_shape=jax.ShapeDtypeStruct((b, s, heads * LANES), BF),
        grid=(b, heads, s // t),
        in_specs=[
            pl.BlockSpec((1, t, dk), lambda bi, h, i: (bi, i, h)),
            pl.BlockSpec((1, s, dk), lambda bi, h, i: (bi, 0, h)),
            pl.BlockSpec((1, VT_ROWS, s), lambda bi, h, i: (bi, h, 0)),
        ],
        out_specs=pl.BlockSpec((1, t, LANES), lambda bi, h, i: (bi, i, h)),
        compiler_params=_params(("parallel", "parallel", "arbitrary")),
    )(q, k, vt)


INT_MAX = 2 ** 31 - 1
VALUE_STEPS = 24


def _float_to_key(x):
    bits = pltpu.bitcast(x, I32)
    return jnp.where(bits < 0, bits ^ 0x7FFFFFFF, bits)


def _key_to_float(key):
    return pltpu.bitcast(jnp.where(key < 0, key ^ 0x7FFFFFFF, key), F32)


DSA_NQ = DSA_HEADS * DSA_HEAD_DIM
DSA_NKV = DSA_KV_HEADS * DSA_HEAD_DIM
DSA_NQI = IDX_HEADS * IDX_DIM
DSA_COLS = DSA_NQ + 2 * DSA_NKV + DSA_NQI + 2 * LANES


def _dsa_in_kernel(x_ref, g_ref, sc_ref, sh_ref, w_ref, gk_ref, cos_a_ref, sin_a_ref, cos_i_ref,
                   sin_i_ref, q_ref, k_ref, v_ref, qi_ref, ki_ref, wi_ref, stage_ref):
    h = _norm_mod(x_ref[0], g_ref[...], sc_ref[0], sh_ref[0]).astype(BF)
    cos_a, sin_a = cos_a_ref[0], sin_a_ref[0]
    cos_i, sin_i = cos_i_ref[0], sin_i_ref[0]
    scale = DSA_HEAD_DIM ** -0.5 * LOG2_E
    ha, hi = DSA_ROT_DIM // 2, IDX_ROT_DIM // 2
    for hd in range(DSA_HEADS):
        c0 = hd * LANES
        y = _dot(h, w_ref[:, c0:c0 + LANES])
        q_ref[0, :, c0:c0 + LANES] = (_rope(y, cos_a, sin_a, ha, LANES) * scale).astype(BF)
    for hd in range(DSA_KV_HEADS):
        c0 = hd * LANES
        y = _dot(h, w_ref[:, DSA_NQ + c0:DSA_NQ + c0 + LANES])
        k_ref[0, :, c0:c0 + LANES] = _rope(y, cos_a, sin_a, ha, LANES).astype(BF)
    off = DSA_NQ + DSA_NKV
    for hd in range(DSA_KV_HEADS):
        _store_vt(v_ref, hd, _dot(h, w_ref[:, off + hd * LANES:off + (hd + 1) * LANES]), stage_ref)
    off += DSA_NKV
    for p in range(DSA_NQI // LANES):
        c0 = p * LANES
        y = _dot(h, w_ref[:, off + c0:off + c0 + LANES])
        qi_ref[0, :, c0:c0 + LANES] = _rope(y, cos_i, sin_i, hi, IDX_DIM).astype(BF)
    off += DSA_NQI
    y = _dot(h, w_ref[:, off:off + LANES])
    y = y * lax.rsqrt(jnp.sum(y * y, axis=-1, keepdims=True) * (1.0 / IDX_DIM) + EPS) * gk_ref[...]
    y = _rope(y, cos_i, sin_i, hi, IDX_DIM)
    ki_ref[0] = (y + pltpu.roll(y, IDX_DIM, 1)).astype(BF)
    off += LANES
    wi_ref[0] = _dot(h, w_ref[:, off:off + LANES]) * (IDX_HEADS ** -0.5 * IDX_DIM ** -0.5)


def _dsa_in(x, g, sc, sh, w, gk, cos_a, sin_a, cos_i, sin_i, tm=512):
    b, s, d = x.shape
    row = lambda bi, i: (bi, i, 0)
    vec = lambda bi, i: (bi, 0, 0)
    full = lambda bi, i: (0, 0)
    tab = pl.BlockSpec((1, tm, LANES), row)
    nvt = DSA_KV_HEADS * VT_ROWS

    def spec(n):
        return pl.BlockSpec((1, tm, n), row)

    def sds(n, dt=BF):
        return jax.ShapeDtypeStruct((b, s, n), dt)

    return pl.pallas_call(
        _dsa_in_kernel,
        name="dsa_in",
        out_shape=(sds(DSA_NQ), sds(DSA_NKV), jax.ShapeDtypeStruct((b, nvt, s), BF), sds(DSA_NQI),
                   sds(LANES), sds(LANES, F32)),
        grid=(b, s // tm),
        in_specs=[
            pl.BlockSpec((1, tm, d), row),
            pl.BlockSpec((1, d), full),
            pl.BlockSpec((1, 1, d), vec),
            pl.BlockSpec((1, 1, d), vec),
            pl.BlockSpec(w.shape, full),
            pl.BlockSpec(gk.shape, full),
            tab, tab, tab, tab,
        ],
        out_specs=(spec(DSA_NQ), spec(DSA_NKV), pl.BlockSpec((1, nvt, tm), lambda bi, i: (bi, 0, i)),
                   spec(DSA_NQI), spec(LANES), spec(LANES)),
        scratch_shapes=[pltpu.VMEM((tm, LANES), F32)],
        compiler_params=_params(("parallel", "parallel")),
    )(x, g, sc, sh, w, gk, cos_a, sin_a, cos_i, sin_i)


def _dsa_attn_kernel(q_ref, qi_ref, wi_ref, ki_ref, k_ref, vt_ref, o_ref, key_scr, *, tq, tk, n_sel):
    i = pl.program_id(1)
    n_chunks = (i * tq + tq + tk - 1) // tk
    group = DSA_HEADS // DSA_KV_HEADS
    key_io = lax.broadcasted_iota(I32, (tk, tq), 0)
    qry = lax.broadcasted_iota(I32, (1, tq), 1) + i * tq
    lane = lax.broadcasted_iota(I32, (tq, LANES), 1)

    qi = qi_ref[0]
    zero = jnp.zeros_like(qi[:, :LANES])
    stacked = []
    for hd in range(IDX_HEADS):
        pair = qi[:, (hd // 2) * LANES:(hd // 2 + 1) * LANES]
        low = (hd % 2) == 0
        stacked.append(jnp.where((lane < IDX_DIM) == low, pair, zero))
    q_all = jnp.concatenate(stacked, axis=0)
    w_t = wi_ref[0].T

    def fold8(x, op):
        for shift in (4, 2, 1):
            x = op(x, pltpu.roll(x, shift, 0))
        return x

    def score_chunk(c, carry):
        k_min, k_max = carry
        start = pl.multiple_of(c * tk, tk)
        logits = _dot_t(ki_ref[0, pl.ds(start, tk), :], q_all)
        score = jnp.zeros((tk, tq), F32)
        for hd in range(IDX_HEADS):
            score = score + jnp.maximum(logits[:, hd * tq:(hd + 1) * tq], 0.0) * w_t[hd:hd + 1]
        key = _float_to_key(score)
        key = jnp.where(score == 0.0, 0, key)
        causal = key_io + start <= qry
        key_scr[pl.ds(start, tk), :] = jnp.where(causal, key, INT_MIN)
        k_min = jnp.minimum(k_min, jnp.min(
            jnp.where(causal, key, INT_MAX).reshape(tk // 8, 8, tq), axis=0))
        k_max = jnp.maximum(k_max, jnp.max(
            jnp.where(causal, key, INT_MIN).reshape(tk // 8, 8, tq), axis=0))
        return k_min, k_max

    k_min, k_max = lax.fori_loop(
        0, n_chunks, score_chunk,
        (jnp.full((8, tq), INT_MAX, I32), jnp.full((8, tq), INT_MIN, I32)))
    k_min, k_max = fold8(k_min, jnp.minimum), fold8(k_max, jnp.maximum)

    tc = min(tk, 512)
    n_pieces = (i * tq + tq + tc - 1) // tc

    def count(pred):
        def chunk(c, cnt):
            start = pl.multiple_of(c * tc, tc)
            key = key_scr[pl.ds(start, tc), :].reshape(tc // 8, 8, tq)
            krow = (lax.broadcasted_iota(I32, (tc, tq), 0) + start).reshape(tc // 8, 8, tq)
            return cnt + jnp.sum(pred(key, krow).astype(I32), axis=0)
        cnt = lax.fori_loop(0, n_pieces, chunk, jnp.zeros((8, tq), I32))
        return fold8(cnt, jnp.add)

    def search_step(it, lo, hi, n_lo, n_hi, done):
        f_lo, f_hi = _key_to_float(lo), _key_to_float(hi)
        frac = ((n_lo - n_sel).astype(F32) - 0.5) / (n_lo - n_hi).astype(F32)
        by_count = _float_to_key(f_lo + frac * (f_hi - f_lo))
        by_value = _float_to_key(0.5 * f_lo + 0.5 * f_hi)
        by_key = (lo >> 1) + (hi >> 1) + (lo & hi & 1)
        cand = jnp.where(it >= VALUE_STEPS, by_key, jnp.where(it % 2 == 0, by_count, by_value))
        cand = jnp.where(it < 2, it, cand)
        cand = jnp.minimum(jnp.maximum(cand, lo + 1), hi - 1)
        n_c = count(lambda key, krow: key >= cand)
        live = done == 0
        up = jnp.logical_and(live, n_c >= n_sel)
        down = jnp.logical_and(live, n_c < n_sel)
        lo, n_lo = jnp.where(up, cand, lo), jnp.where(up, n_c, n_lo)
        hi, n_hi = jnp.where(down, cand, hi), jnp.where(down, n_c, n_hi)
        done = jnp.where(jnp.logical_or(n_lo == n_sel, hi == lo + 1), 1, done)
        return lo, hi, n_lo, n_hi, done

    def search_cond(carry):
        return jnp.logical_and(carry[0] < VALUE_STEPS + 34, jnp.min(carry[-1]) == 0)

    def search_body(carry):
        it, state = carry[0], carry[1:]
        for u in range(4):
            state = search_step(it + u, *state)
        return (it + 4,) + state

    n_lo0 = jnp.zeros((8, tq), I32) + (qry + 1)
    hi0 = k_max + 1
    done0 = jnp.where(jnp.logical_or(n_lo0 <= n_sel, hi0 == k_min + 1), 1, 0)
    res = lax.while_loop(search_cond, search_body,
                         (jnp.int32(0), k_min, hi0, n_lo0, jnp.zeros((8, tq), I32), done0))
    thr, n_at = res[1], res[3]
    tie_bits = key_scr.shape[0].bit_length()

    def break_ties(_):
        want = n_sel - count(lambda key, krow: key > thr)

        def cut_bit(b, cut):
            cand = cut + lax.shift_left(jnp.int32(1), tie_bits - 1 - b)
            n_tie = count(lambda key, krow: jnp.logical_and(key == thr, krow < cand))
            return jnp.where(n_tie <= want, cand, cut)
        cut = lax.fori_loop(0, tie_bits, cut_bit, jnp.zeros((8, tq), I32))[:1]

        def demote(c, _):
            start = pl.multiple_of(c * tc, tc)
            key = key_scr[pl.ds(start, tc), :]
            krow = lax.broadcasted_iota(I32, (tc, tq), 0) + start
            drop = jnp.logical_and(key == thr[:1], krow >= cut)
            key_scr[pl.ds(start, tc), :] = jnp.where(drop, key - 1, key)
            return 0
        lax.fori_loop(0, n_pieces, demote, 0)
        return jnp.int32(0)

    crowded = jnp.max(jnp.where(jnp.logical_and(n_at > n_sel, thr > INT_MIN), 1, 0)) > 0
    lax.cond(crowded, break_ties, lambda _: jnp.int32(0), 0)
    thr = thr[:1]

    q = q_ref[0]
    q_g = [jnp.concatenate([q[:, (g * group + n) * LANES:(g * group + n + 1) * LANES]
                            for n in range(group)], axis=0) for g in range(DSA_KV_HEADS)]

    def scores(c, g):
        start = pl.multiple_of(c * tk, tk)
        return _dot_t(k_ref[0, pl.ds(start, tk), g * LANES:(g + 1) * LANES], q_g[g])

    def attn_chunk(c, carry):
        start = pl.multiple_of(c * tk, tk)
        s_t = [scores(c, g) for g in range(DSA_KV_HEADS)]
        bias = jnp.where(key_scr[pl.ds(start, tk), :] >= thr, 0.0, NEG_BIG)
        bias = jnp.concatenate([bias] * group, axis=1)
        out = []
        for g in range(DSA_KV_HEADS):
            m, acc = carry[2 * g:2 * g + 2]
            out.extend(_softmax_step(
                s_t[g] + bias, m, acc, vt_ref[0, g * VT_ROWS:(g + 1) * VT_ROWS, pl.ds(start, tk)]))
        return tuple(out)

    init = (jnp.full((1, group * tq), M_INIT, F32), jnp.zeros((VT_ROWS, group * tq), F32))
    res = lax.fori_loop(0, n_chunks, attn_chunk, init * DSA_KV_HEADS)
    for g in range(DSA_KV_HEADS):
        acc = res[2 * g + 1]
        for n in range(group):
            hd = g * group + n
            o_ref[0, :, hd * LANES:(hd + 1) * LANES] = _softmax_finish(
                acc[:, n * tq:(n + 1) * tq]).astype(BF)


def _dsa_attn(q, qi, wi, ki, k, vt, tq=256, tk=1024):
    b, s, _ = q.shape
    n_sel = min(DSA_TOPK_MAX, s // 4)
    tk = min(tk, s)
    row = lambda bi, i: (bi, i, 0)
    whole = lambda bi, i: (bi, 0, 0)
    return pl.pallas_call(
        functools.partial(_dsa_attn_kernel, tq=tq, tk=tk, n_sel=n_sel),
        name="dsa_attn",
        out_shape=jax.ShapeDtypeStruct((b, s, DSA_NQ), BF),
        grid=(b, s // tq),
        in_specs=[
            pl.BlockSpec((1, tq, DSA_NQ), row),
            pl.BlockSpec((1, tq, DSA_NQI), row),
            pl.BlockSpec((1, tq, LANES), row),
            pl.BlockSpec((1, s, LANES), whole),
            pl.BlockSpec((1, s, DSA_NKV), whole),
            pl.BlockSpec((1, DSA_KV_HEADS * VT_ROWS, s), whole),
        ],
        out_specs=pl.BlockSpec((1, tq, DSA_NQ), row),
        scratch_shapes=[pltpu.VMEM((s, tq), I32)],
        compiler_params=_params(("parallel", "arbitrary")),
    )(q, qi, wi, ki, k, vt)


def _ffn_kernel(x_ref, g_ref, sc_ref, sh_ref, wg_ref, wu_ref, wd_ref, gp_ref, gate_ref, xo_ref,
                h_scr, acc_scr):
    e = pl.program_id(2)

    @pl.when(e == 0)
    def _():
        h_scr[...] = _norm_mod(x_ref[0], g_ref[...], sc_ref[0], sh_ref[0]).astype(BF)
        acc_scr[...] = jnp.zeros(acc_scr.shape, F32)

    h = h_scr[...]
    a = _dot(h, wg_ref[0])
    u = _dot(h, wu_ref[0])
    act = (a / (1.0 + jnp.exp(-a)) * u).astype(BF)
    acc_scr[...] += _dot(act, wd_ref[0])

    @pl.when(e == pl.num_programs(2) - 1)
    def _():
        xo_ref[0] = x_ref[0] + gate_ref[0] * _rms(acc_scr[...], gp_ref[...])


def _ffn(x, g, sc, sh, w_gate, w_up, w_down, g_post, gate, tm=512):
    b, s, d = x.shape
    n_e, _, f = w_gate.shape
    row = lambda bi, i, e: (bi, i, 0)
    vec = lambda bi, i, e: (bi, 0, 0)
    full = lambda bi, i, e: (0, 0)
    slab = lambda bi, i, e: (e, 0, 0)
    return pl.pallas_call(
        _ffn_kernel,
        name="ffn",
        out_shape=jax.ShapeDtypeStruct(x.shape, F32),
        grid=(b, s // tm, n_e),
        in_specs=[
            pl.BlockSpec((1, tm, d), row),
            pl.BlockSpec((1, d), full),
            pl.BlockSpec((1, 1, d), vec),
            pl.BlockSpec((1, 1, d), vec),
            pl.BlockSpec((1, d, f), slab),
            pl.BlockSpec((1, d, f), slab),
            pl.BlockSpec((1, f, d), slab),
            pl.BlockSpec((1, d), full),
            pl.BlockSpec((1, 1, d), vec),
        ],
        out_specs=pl.BlockSpec((1, tm, d), row),
        scratch_shapes=[pltpu.VMEM((tm, d), BF), pltpu.VMEM((tm, d), F32)],
        compiler_params=_params(("parallel", "parallel", "arbitrary")),
    )(x, g, sc, sh, w_gate, w_up, w_down, g_post, gate)


MOE_TILE = 1024
MOE_ROWS = 256


def _moe_kernel(x_ref, g_ref, sc_ref, sh_ref, wr_ref, br_ref, wg_ref, wu_ref, wd_ref, gp_ref,
                gate_ref, xo_ref, h_scr, gate_c, rank_c, rank_r, acc_scr, *, rows):
    e = pl.program_id(2)
    t = h_scr.shape[0]
    lane = lax.broadcasted_iota(I32, (t, LANES), 1)

    @pl.when(e == 0)
    def _():
        h32 = _norm_mod(x_ref[0], g_ref[...], sc_ref[0], sh_ref[0])
        h = h32.astype(BF)
        h_scr[...] = h
        acc_scr[...] = jnp.zeros(acc_scr.shape, F32)
        h_lo = (h32 - h.astype(F32)).astype(BF)
        logits = _dot(h, wr_ref[0]) + (_dot(h, wr_ref[1]) + _dot(h_lo, wr_ref[0]))
        logits = jnp.where(lane < N_EXPERTS, logits + br_ref[...], -jnp.inf)
        m1 = jnp.max(logits, axis=-1, keepdims=True)
        i1 = jnp.min(jnp.where(logits == m1, lane, LANES), axis=-1, keepdims=True)
        rest = jnp.where(lane == i1, -jnp.inf, logits)
        m2 = jnp.max(rest, axis=-1, keepdims=True)
        i2 = jnp.min(jnp.where(rest == m2, lane, LANES), axis=-1, keepdims=True)
        e2 = jnp.exp(m2 - m1)
        w1 = 1.0 / (1.0 + e2)
        gate_c[...] = jnp.where(lane == i1, w1, 0.0) + jnp.where(lane == i2, e2 * w1, 0.0)
        sel = jnp.logical_or(lane == i1, lane == i2)
        earlier = (lax.broadcasted_iota(I32, (t, t), 1) < lax.broadcasted_iota(I32, (t, t), 0))
        rank = _dot(earlier.astype(BF), jnp.where(sel, 1.0, 0.0).astype(BF))
        rank = jnp.where(sel, rank, -1.0)
        rank_c[...] = rank
        rank_r[...] = rank.T

    rank_row = rank_r[pl.ds(e, 1), :]
    on_e = lane == e
    rank_col = jnp.sum(jnp.where(on_e, rank_c[...], 0.0), axis=-1, keepdims=True)
    gate_col = jnp.sum(jnp.where(on_e, gate_c[...], 0.0), axis=-1, keepdims=True)
    n_tokens = jnp.max(rank_row).astype(I32) + 1

    def expert_pass(first_row, rows):
        base = first_row.astype(F32)
        row_io = lax.broadcasted_iota(I32, (rows, t), 0).astype(F32) + base
        pick = jnp.where(rank_row == row_io, 1.0, 0.0).astype(BF)
        hs = _dot(pick, h_scr[...]).astype(BF)
        a = _dot(hs, wg_ref[0])
        u = _dot(hs, wu_ref[0])
        act = (a / (1.0 + jnp.exp(-a)) * u).astype(BF)
        y = _dot(act, wd_ref[0]).astype(BF)
        col_io = lax.broadcasted_iota(I32, (t, rows), 1).astype(F32) + base
        place = jnp.where(rank_col == col_io, 1.0, 0.0).astype(BF)
        acc_scr[...] += gate_col * _dot(place, y)

    half = rows // 2
    n_half = (n_tokens + half - 1) // half

    def full_pass(p, _):
        expert_pass(p * rows, rows)
        return 0

    lax.fori_loop(0, n_half // 2, full_pass, 0)

    @pl.when(n_half % 2 == 1)
    def _():
        expert_pass((n_half // 2) * rows, half)

    @pl.when(e == pl.num_programs(2) - 1)
    def _():
        xo_ref[0] = x_ref[0] + gate_ref[0] * _rms(acc_scr[...], gp_ref[...])


def _moe(x, g, sc, sh, w_router, b_router, w_gate, w_up, w_down, g_post, gate):
    b, s, d = x.shape
    n_e, _, f = w_gate.shape
    tm = min(MOE_TILE, s)
    row = lambda bi, i, e: (bi, i, 0)
    vec = lambda bi, i, e: (bi, 0, 0)
    full = lambda bi, i, e: (0, 0)
    slab = lambda bi, i, e: (e, 0, 0)
    once = pl.Buffered(1)
    return pl.pallas_call(
        functools.partial(_moe_kernel, rows=min(MOE_ROWS, tm)),
        name="moe",
        out_shape=jax.ShapeDtypeStruct(x.shape, F32),
        grid=(b, s // tm, n_e),
        in_specs=[
            pl.BlockSpec((1, tm, d), row, pipeline_mode=once),
            pl.BlockSpec((1, d), full),
            pl.BlockSpec((1, 1, d), vec),
            pl.BlockSpec((1, 1, d), vec),
            pl.BlockSpec((2, d, LANES), lambda bi, i, e: (0, 0, 0)),
            pl.BlockSpec((1, LANES), full),
            pl.BlockSpec((1, d, f), slab),
            pl.BlockSpec((1, d, f), slab),
            pl.BlockSpec((1, f, d), slab),
            pl.BlockSpec((1, d), full),
            pl.BlockSpec((1, 1, d), vec),
        ],
        out_specs=pl.BlockSpec((1, tm, d), row, pipeline_mode=once),
        scratch_shapes=[
            pltpu.VMEM((tm, d), BF),
            pltpu.VMEM((tm, LANES), F32),
            pltpu.VMEM((tm, LANES), F32),
            pltpu.VMEM((LANES, tm), F32),
            pltpu.VMEM((tm, d), F32),
        ],
        compiler_params=_params(("parallel", "parallel", "arbitrary")),
    )(x, g, sc, sh, w_router, b_router, w_gate, w_up, w_down, g_post, gate)


def _rope_tables(positions, rot_dim, group):
    half = rot_dim // 2
    inv_freq = ROPE_THETA ** (-jnp.arange(0, rot_dim, 2, dtype=F32) / rot_dim)
    ang = positions.astype(F32)[..., None] * inv_freq
    cos, sin = jnp.cos(ang), jnp.sin(ang)
    pad = group - rot_dim
    ones = jnp.ones(cos.shape[:-1] + (pad,), F32)
    zeros = jnp.zeros(cos.shape[:-1] + (pad,), F32)
    cos_t = jnp.concatenate([cos, cos, ones], axis=-1)
    sin_t = jnp.concatenate([-sin, sin, zeros], axis=-1)
    reps = LANES // group
    return jnp.tile(cos_t, (1, 1, reps)), jnp.tile(sin_t, (1, 1, reps))


def _pad_cols(w, n):
    return jnp.pad(w, ((0, 0), (0, n - w.shape[1])))


def _split_gu(w_gu, f):
    return w_gu[..., :f].astype(BF), w_gu[..., f:].astype(BF)


def kernel(x, c, positions, ada_w, ada_b, norm_g, sb_w_in, sb_w_out, mla_w_in, mla_g_q, mla_w_q_up,
           mla_g_kv, mla_w_kv_up, mla_w_out, dsa_w_in, dsa_g_kidx, dsa_w_out, ffn_w_gu, ffn_w_down,
           moe_w_router, moe_b_router, moe_w_gu, moe_w_down):
    depth = ada_w.shape[0]
    b, s, d = x.shape
    mod = _ada(c, ada_w, ada_b)
    cos_m, sin_m = _rope_tables(positions, MLA_ROPE_DIM, LANES)
    cos_a, sin_a = _rope_tables(positions, DSA_ROT_DIM, LANES)
    cos_i, sin_i = _rope_tables(positions, IDX_ROT_DIM, IDX_DIM)
    counters = [0, 0, 0]
    for layer in range(depth):
        sh_m, sc_m, g_m, sh_f, sc_f, g_f = [
            mod[layer, :, n * d:(n + 1) * d].reshape(b, 1, d) for n in range(6)]
        gn = norm_g[layer].reshape(4, 1, d)
        kind = layer % N_MIXERS
        j = counters[kind]
        counters[kind] += 1
        if kind == 0:
            qkv = _sb_in(x, gn[0], sc_m, sh_m, sb_w_in[j].astype(BF))
            o = _sb_attn(qkv)
            w_out = sb_w_out[j]
        elif kind == 1:
            w_in = _pad_cols(mla_w_in[j], 4 * LANES).astype(BF)
            w_q = mla_w_q_up[j].reshape(MLA_Q_LORA, MLA_HEADS, MLA_NOPE_DIM + MLA_ROPE_DIM)
            w_q = jnp.pad(w_q, ((0, 0), (0, 0), (0, MLA_QK_PAD - w_q.shape[-1])))
            w_q = w_q.reshape(MLA_Q_LORA, MLA_HEADS * MLA_QK_PAD).astype(BF)
            w_kv = mla_w_kv_up[j].reshape(MLA_KV_LORA, MLA_HEADS, MLA_NOPE_DIM + MLA_V_DIM)
            w_kv = jnp.concatenate([w_kv[..., :MLA_NOPE_DIM].reshape(MLA_KV_LORA, -1),
                                    w_kv[..., MLA_NOPE_DIM:].reshape(MLA_KV_LORA, -1)], axis=1)
            q, k, v = _mla_in(x, gn[0], sc_m, sh_m, w_in, mla_g_q[j].reshape(1, -1), w_q,
                              mla_g_kv[j].reshape(1, -1), w_kv.astype(BF), cos_m, sin_m)
            o = _flash(q, k, v, MLA_HEADS, MLA_QK_PAD)
            w_out = mla_w_out[j]
        else:
            w = dsa_w_in[j]
            n_main = DSA_NQ + 2 * DSA_NKV + DSA_NQI
            w = jnp.concatenate([w[:, :n_main], _pad_cols(w[:, n_main:n_main + IDX_DIM], LANES),
                                 _pad_cols(w[:, n_main + IDX_DIM:], LANES)], axis=1).astype(BF)
            gk = _pad_cols(dsa_g_kidx[j].reshape(1, -1), LANES)
            q, k, v, qi, ki, wi = _dsa_in(x, gn[0], sc_m, sh_m, w, gk, cos_a, sin_a, cos_i, sin_i)
            o = _dsa_attn(q, qi, wi, ki, k, v)
            w_out = dsa_w_out[j]
        x = _out_proj(o, w_out.astype(BF), x, gn[1], g_m)
        f = layer // 2
        if layer % 2 == 0:
            half = D_FF // 2
            w_gate, w_up = _split_gu(ffn_w_gu[f], D_FF)
            w_gate = w_gate.reshape(d, 2, half).transpose(1, 0, 2)
            w_up = w_up.reshape(d, 2, half).transpose(1, 0, 2)
            w_down = ffn_w_down[f].astype(BF).reshape(2, half, d)
            x = _ffn(x, gn[2], sc_f, sh_f, w_gate, w_up, w_down, gn[3], g_f)
        else:
            w_gate, w_up = _split_gu(moe_w_gu[f], D_FF_EXPERT)
            w_r = _pad_cols(moe_w_router[f], LANES)
            w_r_hi = w_r.astype(BF)
            w_r = jnp.stack([w_r_hi, (w_r - w_r_hi.astype(F32)).astype(BF)])
            b_r = _pad_cols(moe_b_router[f].reshape(1, -1), LANES)
            x = _moe(x, gn[2], sc_f, sh_f, w_r, b_r, w_gate, w_up, moe_w_down[f].astype(BF), gn[3],
                     g_f)
    return x
```
